```python
import math
import jax
import jax.numpy as jnp
from jax import lax
import numpy as np

D_MODEL = 4096
BATCH = 4
SEQ = 2048
DEPTH = 2
DEC_BATCH = 32
DEC_SEQ = 4
PAST_LEN = 16384
PAGE_SIZE = 128

HEAD_DIM = 64
ATT_WIDTH = D_MODEL // 4
N_Q_HEADS = ATT_WIDTH // HEAD_DIM
N_KV_HEADS = max(1, N_Q_HEADS // 8)
Q_PER_KV = N_Q_HEADS // N_KV_HEADS
WINDOW = 128
ROPE_THETA = 10000.0
MLSTM_WIDTH = D_MODEL // 2
N_MLSTM_HEADS = 4
MLSTM_DV = MLSTM_WIDTH // N_MLSTM_HEADS
MLSTM_DK = MLSTM_DV // 2
MLSTM_CHUNK = 64
CONV_CH = D_MODEL - ATT_WIDTH - MLSTM_WIDTH
CONV_WIDTH = 31
D_MIX = ATT_WIDTH + MLSTM_WIDTH + CONV_CH
D_IN = (N_Q_HEADS + 2 * N_KV_HEADS) * HEAD_DIM + N_MLSTM_HEADS * (2 * MLSTM_DK + 2 * MLSTM_DV + 2) + 2 * CONV_CH
D_FF = 4 * D_MODEL
NORM_EPS = 1e-6

kernel_name = 'hymba_swa_mlstm_conformer_decode_step'


def _split_points():
    sizes = [N_Q_HEADS * HEAD_DIM, N_KV_HEADS * HEAD_DIM, N_KV_HEADS * HEAD_DIM,
             N_MLSTM_HEADS * MLSTM_DK, N_MLSTM_HEADS * MLSTM_DK, N_MLSTM_HEADS * MLSTM_DV,
             N_MLSTM_HEADS, N_MLSTM_HEADS, N_MLSTM_HEADS * MLSTM_DV, CONV_CH, CONV_CH]
    return [int(s) for s in np.cumsum(sizes)[:-1]]


def _rmsnorm(x, g):
    x32 = x.astype(jnp.float32)
    y = x32 * lax.rsqrt(jnp.mean(x32 * x32, axis=-1, keepdims=True) + NORM_EPS)
    return (y * g.astype(jnp.float32)).astype(x.dtype)


def _layernorm(x, g, b):
    mu = jnp.mean(x, axis=-1, keepdims=True)
    var = jnp.mean(jnp.square(x - mu), axis=-1, keepdims=True)
    return (x - mu) * lax.rsqrt(var + NORM_EPS) * g.astype(jnp.float32) + b.astype(jnp.float32)


def _rotary(x, pos):
    half = HEAD_DIM // 2
    inv_freq = ROPE_THETA ** (-jnp.arange(half, dtype=jnp.float32) / half)
    ang = pos.astype(jnp.float32)[:, None] * inv_freq[None, :]
    cos = jnp.cos(ang)[None, :, None, :]
    sin = jnp.sin(ang)[None, :, None, :]
    x32 = x.astype(jnp.float32)
    x1, x2 = x32[..., :half], x32[..., half:]
    return jnp.concatenate([x1 * cos - x2 * sin, x2 * cos + x1 * sin], axis=-1)


def _window_attention(q, k_ext, v_ext, pos0, sinks):
    N, L = q.shape[:2]
    W = k_ext.shape[1] - L
    pos_q = pos0 + jnp.arange(L, dtype=jnp.int32)
    pos_k = pos0 - W + jnp.arange(W + L, dtype=jnp.int32)
    if L % W == 0:
        nb = L // W
        qb = q.reshape(N, nb, W, N_KV_HEADS, Q_PER_KV, HEAD_DIM)
        kb = k_ext.reshape(N, nb + 1, W, N_KV_HEADS, HEAD_DIM)
        vb = v_ext.reshape(N, nb + 1, W, N_KV_HEADS, HEAD_DIM)
        kb = jnp.concatenate([kb[:, :-1], kb[:, 1:]], axis=2)
        vb = jnp.concatenate([vb[:, :-1], vb[:, 1:]], axis=2)
        pq = pos_q.reshape(nb, W)
        pk = pos_k.reshape(nb + 1, W)
        pk = jnp.concatenate([pk[:-1], pk[1:]], axis=1)
    else:
        qb = q.reshape(N, 1, L, N_KV_HEADS, Q_PER_KV, HEAD_DIM)
        kb = k_ext[:, None]
        vb = v_ext[:, None]
        pq = pos_q[None]
        pk = pos_k[None]
    scores = jnp.einsum('nbqkgd,nbskd->nbkgqs', qb, kb, preferred_element_type=jnp.float32) * (HEAD_DIM ** -0.5)
    dist = pq[:, :, None] - pk[:, None, :]
    mask = (dist >= 0) & (dist <= WINDOW) & (pk[:, None, :] >= 0)
    scores = jnp.where(mask[None, :, None, None], scores, -jnp.inf)
    sink = sinks.astype(jnp.float32).reshape(N_KV_HEADS, Q_PER_KV)[None, None, :, :, None]
    m = jnp.maximum(scores.max(axis=-1), sink)
    p = jnp.exp(scores - m[..., None])
    denom = p.sum(axis=-1) + jnp.exp(sink - m)
    out = jnp.einsum('nbkgqs,nbskd->nbqkgd', p / denom[..., None], vb.astype(jnp.float32))
    return out.reshape(N, L, N_Q_HEADS * HEAD_DIM)


def _mlstm(q, k, v, log_i, log_f, C0, n0, m0):
    N, L, H, DK = q.shape
    DV = v.shape[-1]
    cs = math.gcd(L, MLSTM_CHUNK)
    nc = L // cs

    def blocks(a):
        a = a.reshape((N, nc, cs) + a.shape[2:])
        return jnp.moveaxis(jnp.moveaxis(a, 1, 0), 3, 2)

    tril = jnp.tril(jnp.ones((cs, cs), dtype=bool))

    def step(carry, blk):
        C, n, m = carry
        qc, kc, vc, li, lf = blk
        b = jnp.cumsum(lf, axis=-1)
        log_d = jnp.where(tril, b[..., :, None] - b[..., None, :] + li[..., None, :], -jnp.inf)
        inter = b + m[..., None]
        m_row = jnp.maximum(inter, log_d.max(axis=-1))
        s = jnp.einsum('nhtd,nhsd->nhts', qc, kc) * jnp.exp(log_d - m_row[..., None])
        w_inter = jnp.exp(inter - m_row)
        num = jnp.einsum('nhts,nhsv->nhtv', s, vc) + w_inter[..., None] * jnp.einsum('nhvd,nhtd->nhtv', C, qc)
        den = s.sum(axis=-1) + w_inter * jnp.einsum('nhd,nhtd->nht', n, qc)
        h = num / jnp.maximum(jnp.abs(den), jnp.exp(-m_row))[..., None]
        b_last = b[..., -1]
        g = b_last[..., None] - b + li
        m_new = jnp.maximum(b_last + m, g.max(axis=-1))
        decay = jnp.exp(b_last + m - m_new)
        wk = jnp.exp(g - m_new[..., None])
        C_new = decay[..., None, None] * C + jnp.einsum('nhsv,nhsd->nhvd', wk[..., None] * vc, kc)
        n_new = decay[..., None] * n + jnp.einsum('nhs,nhsd->nhd', wk, kc)
        return (C_new, n_new, m_new), h

    (C1, n1, m1), hs = lax.scan(step, (C0, n0, m0),
                                (blocks(q), blocks(k), blocks(v), blocks(log_i), blocks(log_f)))
    hs = hs.transpose(1, 0, 3, 2, 4).reshape(N, L, H, DV)
    return hs, C1, n1, m1


def _layer(x, pos0, k_buf, v_buf, C0, n0, m0, conv_buf,
           g_pre_mix, w_in, attn_sinks, b_igate, b_fgate, g_mlstm, conv_w, conv_b, g_conv, b_conv,
           w_out, g_post_mix, g_pre_mlp, w_up, w_down, g_post_mlp):
    f32 = jnp.float32
    N, L, _ = x.shape
    h = _rmsnorm(x, g_pre_mix)
    z = jnp.einsum('nld,de->nle', h, w_in)
    qa, ka, va, qm, km, vm, ig, fg, og, ga, gb = jnp.split(z, _split_points(), axis=-1)
    pos = pos0 + jnp.arange(L, dtype=jnp.int32)
    qa = _rotary(qa.reshape(N, L, N_Q_HEADS, HEAD_DIM), pos)
    ka = _rotary(ka.reshape(N, L, N_KV_HEADS, HEAD_DIM), pos).astype(k_buf.dtype)
    va = va.reshape(N, L, N_KV_HEADS, HEAD_DIM).astype(v_buf.dtype)
    k_ext = jnp.concatenate([k_buf, ka], axis=1)
    v_ext = jnp.concatenate([v_buf, va], axis=1)
    out_a = _window_attention(qa, k_ext, v_ext, pos0, attn_sinks)
    wb = k_buf.shape[1]
    k_buf_new = k_ext[:, -wb:]
    v_buf_new = v_ext[:, -wb:]
    qm = qm.reshape(N, L, N_MLSTM_HEADS, MLSTM_DK).astype(f32)
    km = km.reshape(N, L, N_MLSTM_HEADS, MLSTM_DK).astype(f32) * (MLSTM_DK ** -0.5)
    vm = vm.reshape(N, L, N_MLSTM_HEADS, MLSTM_DV).astype(f32)
    log_i = ig.astype(f32) + b_igate.astype(f32)
    log_f = jax.nn.log_sigmoid(fg.astype(f32) + b_fgate.astype(f32))
    hm, C1, n1, m1 = _mlstm(qm, km, vm, log_i, log_f, C0.astype(f32), n0.astype(f32), m0.astype(f32))
    hm = _rmsnorm(hm, g_mlstm) * jax.nn.sigmoid(og.astype(f32)).reshape(N, L, N_MLSTM_HEADS, MLSTM_DV)
    out_b = hm.reshape(N, L, MLSTM_WIDTH)
    u = ga.astype(f32) * jax.nn.sigmoid(gb.astype(f32))
    u_ext = jnp.concatenate([conv_buf.astype(f32), u], axis=1)
    cv = lax.conv_general_dilated(u_ext, conv_w.astype(f32)[:, None, :], window_strides=(1,), padding='VALID',
                                  dimension_numbers=('NWC', 'WIO', 'NWC'), feature_group_count=CONV_CH)
    out_c = jax.nn.silu(_layernorm(cv + conv_b.astype(f32), g_conv, b_conv))
    conv_buf_new = u_ext[:, -(CONV_WIDTH - 1):].astype(conv_buf.dtype)
    mix = jnp.concatenate([out_a, out_b, out_c], axis=-1).astype(x.dtype)
    x = x + _rmsnorm(jnp.einsum('nle,ed->nld', mix, w_out), g_post_mix)
    h2 = _rmsnorm(x, g_pre_mlp)
    ff = jnp.square(jax.nn.relu(jnp.einsum('nld,df->nlf', h2, w_up)))
    x = x + _rmsnorm(jnp.einsum('nlf,fd->nld', ff, w_down), g_post_mlp)
    return x, k_buf_new, v_buf_new, C1, n1, m1, conv_buf_new


def setup_inputs(seed: int = 0) -> dict:
    key = jax.random.key(seed)
    ks = jax.random.split(key, 32)
    f32 = jnp.float32

    def nrm(k, shape, scale):
        return jax.random.normal(k, shape, f32) * scale

    wb = min(WINDOW, PAST_LEN)
    H = N_MLSTM_HEADS
    return {
        'x_prompt': nrm(ks[0], (BATCH, SEQ, D_MODEL), 1.0),
        'x_sample': nrm(ks[1], (DEC_BATCH, DEC_SEQ, D_MODEL), 1.0),
        'cache_win_k': nrm(ks[2], (DEPTH, DEC_BATCH, wb, N_KV_HEADS, HEAD_DIM), 1.0),
        'cache_win_v': nrm(ks[3], (DEPTH, DEC_BATCH, wb, N_KV_HEADS, HEAD_DIM), 1.0),
        'state_mlstm_C': nrm(ks[4], (DEPTH, DEC_BATCH, H, MLSTM_DV, MLSTM_DK), 0.5),
        'state_mlstm_n': nrm(ks[5], (DEPTH, DEC_BATCH, H, MLSTM_DK), 1.0),
        'state_mlstm_m': nrm(ks[6], (DEPTH, DEC_BATCH, H), 1.0),
        'state_conv': nrm(ks[7], (DEPTH, DEC_BATCH, CONV_WIDTH - 1, CONV_CH), 0.5),
        'g_pre_mix': 1.0 + nrm(ks[8], (DEPTH, D_MODEL), 0.02),
        'w_in': nrm(ks[9], (DEPTH, D_MODEL, D_IN), D_MODEL ** -0.5),
        'attn_sinks': nrm(ks[10], (DEPTH, N_Q_HEADS), 0.5),
        'b_igate': nrm(ks[11], (DEPTH, H), 0.1),
        'b_fgate': 3.0 + nrm(ks[12], (DEPTH, H), 0.5),
        'g_mlstm': 1.0 + nrm(ks[13], (DEPTH, H, MLSTM_DV), 0.02),
        'conv_w': nrm(ks[14], (DEPTH, CONV_WIDTH, CONV_CH), CONV_WIDTH ** -0.5),
        'conv_b': nrm(ks[15], (DEPTH, CONV_CH), 0.02),
        'g_conv': 1.0 + nrm(ks[16], (DEPTH, CONV_CH), 0.02),
        'b_conv': nrm(ks[17], (DEPTH, CONV_CH), 0.02),
        'w_out': nrm(ks[18], (DEPTH, D_MIX, D_MODEL), D_MIX ** -0.5),
        'g_post_mix': 1.0 + nrm(ks[19], (DEPTH, D_MODEL), 0.02),
        'g_pre_mlp': 1.0 + nrm(ks[20], (DEPTH, D_MODEL), 0.02),
        'w_up': nrm(ks[21], (DEPTH, D_MODEL, D_FF), D_MODEL ** -0.5),
        'w_down': nrm(ks[22], (DEPTH, D_FF, D_MODEL), D_FF ** -0.5),
        'g_post_mlp': 1.0 + nrm(ks[23], (DEPTH, D_MODEL), 0.02),
    }


def reference(x_prompt, x_sample, cache_win_k, cache_win_v, state_mlstm_C, state_mlstm_n, state_mlstm_m, state_conv,
              g_pre_mix, w_in, attn_sinks, b_igate, b_fgate, g_mlstm, conv_w, conv_b, g_conv, b_conv,
              w_out, g_post_mix, g_pre_mlp, w_up, w_down, g_post_mlp):
    f32 = jnp.float32
    nb_p = x_prompt.shape[0]
    yp = x_prompt
    ys = x_sample
    p_k, p_v, p_C, p_n, p_m, p_cv = [], [], [], [], [], []
    s_k, s_v, s_C, s_n, s_m, s_cv = [], [], [], [], [], []
    for l in range(DEPTH):
        params = (g_pre_mix[l], w_in[l], attn_sinks[l], b_igate[l], b_fgate[l], g_mlstm[l], conv_w[l], conv_b[l],
                  g_conv[l], b_conv[l], w_out[l], g_post_mix[l], g_pre_mlp[l], w_up[l], w_down[l], g_post_mlp[l])
        kb0 = jnp.zeros((nb_p, WINDOW, N_KV_HEADS, HEAD_DIM), cache_win_k.dtype)
        vb0 = jnp.zeros((nb_p, WINDOW, N_KV_HEADS, HEAD_DIM), cache_win_v.dtype)
        C0 = jnp.zeros((nb_p, N_MLSTM_HEADS, MLSTM_DV, MLSTM_DK), f32)
        n0 = jnp.zeros((nb_p, N_MLSTM_HEADS, MLSTM_DK), f32)
        m0 = jnp.zeros((nb_p, N_MLSTM_HEADS), f32)
        cb0 = jnp.zeros((nb_p, CONV_WIDTH - 1, CONV_CH), state_conv.dtype)
        yp, k1, v1, C1, n1, m1, cb1 = _layer(yp, 0, kb0, vb0, C0, n0, m0, cb0, *params)
        p_k.append(k1); p_v.append(v1); p_C.append(C1); p_n.append(n1); p_m.append(m1); p_cv.append(cb1)
        ys, k2, v2, C2, n2, m2, cb2 = _layer(ys, PAST_LEN, cache_win_k[l], cache_win_v[l], state_mlstm_C[l],
                                             state_mlstm_n[l], state_mlstm_m[l], state_conv[l], *params)
        s_k.append(k2); s_v.append(v2); s_C.append(C2); s_n.append(n2); s_m.append(m2); s_cv.append(cb2)
    return (yp, ys,
            jnp.stack(p_k), jnp.stack(p_v), jnp.stack(p_C), jnp.stack(p_n), jnp.stack(p_m), jnp.stack(p_cv),
            jnp.stack(s_k), jnp.stack(s_v), jnp.stack(s_C), jnp.stack(s_n), jnp.stack(s_m), jnp.stack(s_cv))
```

```python
import functools

import numpy as np
import jax
import jax.numpy as jnp
from jax import lax
from jax.experimental import pallas as pl
from jax.experimental.pallas import tpu as pltpu

D_MODEL = 4096
HEAD_DIM = 64
N_Q_HEADS = 16
N_KV_HEADS = 2
Q_PER_KV = N_Q_HEADS // N_KV_HEADS
WINDOW = 128
ROPE_THETA = 10000.0
N_MLSTM_HEADS = 4
MLSTM_DV = 512
MLSTM_DK = 256
CONV_CH = 1024
CONV_WIDTH = 31
NORM_EPS = 1e-6
PAST_LEN = 16384

ATT_WIDTH = N_Q_HEADS * HEAD_DIM
KV_WIDTH = N_KV_HEADS * HEAD_DIM
MLSTM_WIDTH = N_MLSTM_HEADS * MLSTM_DV
COL_QA = 0
COL_KA = COL_QA + ATT_WIDTH
COL_VA = COL_KA + KV_WIDTH
COL_QM = COL_VA + KV_WIDTH
COL_KM = COL_QM + N_MLSTM_HEADS * MLSTM_DK
COL_VM = COL_KM + N_MLSTM_HEADS * MLSTM_DK
COL_GATE = COL_VM + MLSTM_WIDTH
COL_OG = COL_GATE + 2 * N_MLSTM_HEADS
HEAD_COLS = COL_GATE
TAIL_COLS = MLSTM_WIDTH + 2 * CONV_CH

LANES = 128
MLSTM_T = 128
CONV_TB = 128
CONV_HIST = 32
VMEM_LIMIT = 48 * 1024 * 1024


def _largest_tile(total, target, multiple):
    best = None
    for t in range(multiple, min(total, target) + 1, multiple):
        if total % t == 0:
            best = t
    assert best is not None, (total, target, multiple)
    return best


def _cparams(sem):
    return pltpu.CompilerParams(dimension_semantics=sem, vmem_limit_bytes=VMEM_LIMIT)


def _rms(x, g):
    return x * lax.rsqrt(jnp.mean(x * x, axis=-1, keepdims=True) + NORM_EPS) * g


def _norm_kernel(x_ref, g_ref, h_ref):
    h_ref[...] = _rms(x_ref[...], g_ref[...]).astype(h_ref.dtype)


def _norm(x, g):
    m, d = x.shape
    tr = _largest_tile(m, 256, 16)
    return pl.pallas_call(
        _norm_kernel,
        grid=(m // tr,),
        in_specs=[pl.BlockSpec((tr, d), lambda i: (i, 0)),
                  pl.BlockSpec((1, d), lambda i: (0, 0))],
        out_specs=pl.BlockSpec((tr, d), lambda i: (i, 0)),
        out_shape=jax.ShapeDtypeStruct((m, d), jnp.bfloat16),
        compiler_params=_cparams(("parallel",)),
        name="rmsnorm",
    )(x, g.reshape(1, d))


def _resid_kernel(y_ref, x_ref, gpost_ref, gnext_ref, xo_ref, h_ref):
    xn = x_ref[...] + _rms(y_ref[...], gpost_ref[...])
    xo_ref[...] = xn
    h_ref[...] = _rms(xn, gnext_ref[...]).astype(h_ref.dtype)


def _resid_last_kernel(y_ref, x_ref, gpost_ref, xo_ref):
    xo_ref[...] = x_ref[...] + _rms(y_ref[...], gpost_ref[...])


def _resid(y, x, g_post, g_next):
    m, d = x.shape
    tr = _largest_tile(m, 256, 16)
    row = pl.BlockSpec((tr, d), lambda i: (i, 0))
    vec = pl.BlockSpec((1, d), lambda i: (0, 0))
    if g_next is None:
        return pl.pallas_call(
            _resid_last_kernel, grid=(m // tr,),
            in_specs=[row, row, vec], out_specs=row,
            out_shape=jax.ShapeDtypeStruct((m, d), jnp.float32),
            compiler_params=_cparams(("parallel",)), name="resid_last",
        )(y, x, g_post.reshape(1, d)), None
    return pl.pallas_call(
        _resid_kernel, grid=(m // tr,),
        in_specs=[row, row, vec, vec], out_specs=[row, row],
        out_shape=[jax.ShapeDtypeStruct((m, d), jnp.float32),
                   jax.ShapeDtypeStruct((m, d), jnp.bfloat16)],
        compiler_params=_cparams(("parallel",)), name="resid_norm",
    )(y, x, g_post.reshape(1, d), g_next.reshape(1, d))


def _mm_kernel(a_ref, b_ref, o_ref, *scratch, nk, relu2):
    acc_ref = scratch[0] if scratch else o_ref
    k = pl.program_id(2)

    def part():
        return jnp.dot(a_ref[...], b_ref[...].astype(jnp.bfloat16), preferred_element_type=jnp.float32)

    @pl.when(k == 0)
    def _():
        acc_ref[...] = part()

    @pl.when(k > 0)
    def _():
        acc_ref[...] += part()

    if scratch or relu2:
        @pl.when(k == nk - 1)
        def _():
            r = acc_ref[...]
            if relu2:
                r = jnp.square(jnp.maximum(r, 0.0))
            o_ref[...] = r.astype(o_ref.dtype)


def _matmul(a, b, *, n_cols, col_block0=0, tn, out_dtype, relu2=False, name):
    m, kdim = a.shape
    tm = _largest_tile(m, 2080, 16)
    tk = _largest_tile(kdim, 1024, 128)
    nk = kdim // tk
    assert n_cols % tn == 0
    scratch = [] if out_dtype == jnp.float32 else [pltpu.VMEM((tm, tn), jnp.float32)]
    return pl.pallas_call(
        functools.partial(_mm_kernel, nk=nk, relu2=relu2),
        grid=(m // tm, n_cols // tn, nk),
        in_specs=[pl.BlockSpec((tm, tk), lambda i, j, k: (i, k)),
                  pl.BlockSpec((tk, tn), lambda i, j, k: (k, j + col_block0))],
        out_specs=pl.BlockSpec((tm, tn), lambda i, j, k: (i, j)),
        out_shape=jax.ShapeDtypeStruct((m, n_cols), out_dtype),
        scratch_shapes=scratch,
        compiler_params=_cparams(("parallel", "parallel", "arbitrary")),
        name=name,
    )(a, b)


def _rope_tables(pos0, length):
    half = HEAD_DIM // 2
    inv_freq = ROPE_THETA ** (-jnp.arange(half, dtype=jnp.float32) / half)
    pos = (pos0 + jnp.arange(length, dtype=jnp.int32)).astype(jnp.float32)
    ang = pos[:, None] * inv_freq[None, :]
    cos, sin = jnp.cos(ang), jnp.sin(ang)
    reps = LANES // HEAD_DIM
    cos_t = jnp.tile(jnp.concatenate([cos, cos], axis=-1), (1, reps))
    sin_t = jnp.tile(jnp.concatenate([-sin, sin], axis=-1), (1, reps))
    return cos_t, sin_t


def _rope(x, cos_t, sin_t):
    half = HEAD_DIM // 2
    lane = lax.broadcasted_iota(jnp.int32, x.shape, 1)
    swapped = jnp.where((lane % HEAD_DIM) < half,
                        pltpu.roll(x, LANES - half, 1), pltpu.roll(x, half, 1))
    return x * cos_t + swapped * sin_t


def _attend(q_rot, k_ext, v_ext, mask, sink_ref, o_write):
    kb = k_ext.astype(jnp.bfloat16)
    vb = v_ext.astype(jnp.bfloat16)
    scale = HEAD_DIM ** -0.5
    for g in range(N_KV_HEADS):
        kg = kb[:, g * HEAD_DIM:(g + 1) * HEAD_DIM]
        vg = vb[:, g * HEAD_DIM:(g + 1) * HEAD_DIM]
        for hh in range(Q_PER_KV):
            h = g * Q_PER_KV + hh
            qh = (q_rot[:, h * HEAD_DIM:(h + 1) * HEAD_DIM] * scale).astype(jnp.bfloat16)
            s = lax.dot_general(qh, kg, (((1,), (1,)), ((), ())), preferred_element_type=jnp.float32)
            s = jnp.where(mask, s, -jnp.inf)
            sink = sink_ref[0, h]
            m = jnp.maximum(jnp.max(s, axis=-1, keepdims=True), sink)
            p = jnp.exp(s - m)
            denom = jnp.sum(p, axis=-1, keepdims=True) + jnp.exp(sink - m)
            o = jnp.dot(p.astype(jnp.bfloat16), vg, preferred_element_type=jnp.float32)
            o_write(h, o / denom)


def _attn_prompt_kernel(sink_ref, q_ref, kc_ref, kp_ref, vc_ref, vp_ref,
                        cosc_ref, sinc_ref, cosp_ref, sinp_ref,
                        o_ref, kwin_ref, vwin_ref, *, nb):
    j = pl.program_id(1)
    cosc, sinc = cosc_ref[...], sinc_ref[...]
    q = q_ref[...]
    q_rot = jnp.concatenate(
        [_rope(q[:, c * LANES:(c + 1) * LANES], cosc, sinc) for c in range(ATT_WIDTH // LANES)], axis=-1)
    k_cur = _rope(kc_ref[...], cosc, sinc)
    k_prev = _rope(kp_ref[...], cosp_ref[...], sinp_ref[...])
    k_ext = jnp.concatenate([k_prev, k_cur], axis=0)
    v_ext = jnp.concatenate([vp_ref[...], vc_ref[...]], axis=0)
    tq = lax.broadcasted_iota(jnp.int32, (WINDOW, 2 * WINDOW), 0)
    sk = lax.broadcasted_iota(jnp.int32, (WINDOW, 2 * WINDOW), 1)
    mask = ((sk >= tq) & (sk < WINDOW) & (j > 0)) | ((sk >= WINDOW) & ((sk - WINDOW) <= tq))

    def write(h, o):
        o_ref[:, h * HEAD_DIM:(h + 1) * HEAD_DIM] = o.astype(o_ref.dtype)

    _attend(q_rot, k_ext, v_ext, mask, sink_ref, write)

    @pl.when(j == nb - 1)
    def _():
        kwin_ref[0] = k_cur
        vwin_ref[0] = vc_ref[...]


def _attn_prompt(z1, sinks, batch, seq):
    nb = seq // WINDOW
    cos_t, sin_t = _rope_tables(0, seq)
    blk = lambda col: pl.BlockSpec((WINDOW, LANES), lambda n, j: (n * nb + j, col))
    blk_prev = lambda col: pl.BlockSpec((WINDOW, LANES), lambda n, j: (n * nb + jnp.maximum(j - 1, 0), col))
    tab = pl.BlockSpec((WINDOW, LANES), lambda n, j: (j, 0))
    tab_prev = pl.BlockSpec((WINDOW, LANES), lambda n, j: (jnp.maximum(j - 1, 0), 0))
    win = pl.BlockSpec((1, WINDOW, KV_WIDTH), lambda n, j: (n, 0, 0))
    return pl.pallas_call(
        functools.partial(_attn_prompt_kernel, nb=nb),
        grid=(batch, nb),
        in_specs=[pl.BlockSpec(memory_space=pltpu.SMEM),
                  pl.BlockSpec((WINDOW, ATT_WIDTH), lambda n, j: (n * nb + j, 0)),
                  blk(COL_KA // LANES), blk_prev(COL_KA // LANES),
                  blk(COL_VA // LANES), blk_prev(COL_VA // LANES),
                  tab, tab, tab_prev, tab_prev],
        out_specs=[pl.BlockSpec((WINDOW, ATT_WIDTH), lambda n, j: (n * nb + j, 0)), win, win],
        out_shape=[jax.ShapeDtypeStruct((batch * seq, ATT_WIDTH), jnp.bfloat16),
                   jax.ShapeDtypeStruct((batch, WINDOW, KV_WIDTH), jnp.float32),
                   jax.ShapeDtypeStruct((batch, WINDOW, KV_WIDTH), jnp.float32)],
        compiler_params=_cparams(("parallel", "arbitrary")),
        name="attn_prompt",
    )(sinks.reshape(1, N_Q_HEADS), z1, z1, z1, z1, z1, cos_t, sin_t, cos_t, sin_t)


def _attn_sample_kernel(sink_ref, z_ref, ck_ref, cv_ref, cos_ref, sin_ref,
                        o_ref, kout_ref, vout_ref, kext_ref, vext_ref, *, n_new, rows):
    cos_t, sin_t = cos_ref[...], sin_ref[...]
    z = z_ref[0]
    q_rot = jnp.concatenate(
        [_rope(z[:, c * LANES:(c + 1) * LANES], cos_t, sin_t) for c in range(ATT_WIDTH // LANES)], axis=-1)
    k_new = _rope(z[:, COL_KA:COL_KA + KV_WIDTH], cos_t, sin_t)
    v_new = z[:, COL_VA:COL_VA + KV_WIDTH]
    kext_ref[0:WINDOW, :] = ck_ref[0]
    vext_ref[0:WINDOW, :] = cv_ref[0]
    kext_ref[WINDOW:WINDOW + rows, :] = k_new
    vext_ref[WINDOW:WINDOW + rows, :] = v_new
    fill = jnp.zeros((WINDOW - rows, KV_WIDTH), jnp.float32)
    kext_ref[WINDOW + rows:, :] = fill
    vext_ref[WINDOW + rows:, :] = fill
    tq = lax.broadcasted_iota(jnp.int32, (rows, 2 * WINDOW), 0)
    sk = lax.broadcasted_iota(jnp.int32, (rows, 2 * WINDOW), 1)
    mask = ((sk >= tq) & (sk < WINDOW)) | ((sk >= WINDOW) & ((sk - WINDOW) <= tq) & (sk < WINDOW + n_new))

    def write(h, o):
        o_ref[0, :, h * HEAD_DIM:(h + 1) * HEAD_DIM] = o

    _attend(q_rot, kext_ref[...], vext_ref[...], mask, sink_ref, write)
    kout_ref[0] = kext_ref[pl.ds(n_new, WINDOW), :]
    vout_ref[0] = vext_ref[pl.ds(n_new, WINDOW), :]


def _attn_sample(zs, cache_k, cache_v, sinks, n_new):
    s, rows, _ = zs.shape
    cos_t, sin_t = _rope_tables(PAST_LEN, rows)
    seq_blk = lambda r, c: pl.BlockSpec((1, r, c), lambda n: (n, 0, 0))
    tab = pl.BlockSpec((rows, LANES), lambda n: (0, 0))
    return pl.pallas_call(
        functools.partial(_attn_sample_kernel, n_new=n_new, rows=rows),
        grid=(s,),
        in_specs=[pl.BlockSpec(memory_space=pltpu.SMEM),
                  seq_blk(rows, HEAD_COLS), seq_blk(WINDOW, KV_WIDTH), seq_blk(WINDOW, KV_WIDTH), tab, tab],
        out_specs=[seq_blk(rows, ATT_WIDTH), seq_blk(WINDOW, KV_WIDTH), seq_blk(WINDOW, KV_WIDTH)],
        out_shape=[jax.ShapeDtypeStruct((s, rows, ATT_WIDTH), jnp.float32),
                   jax.ShapeDtypeStruct((s, WINDOW, KV_WIDTH), jnp.float32),
                   jax.ShapeDtypeStruct((s, WINDOW, KV_WIDTH), jnp.float32)],
        scratch_shapes=[pltpu.VMEM((2 * WINDOW, KV_WIDTH), jnp.float32),
                        pltpu.VMEM((2 * WINDOW, KV_WIDTH), jnp.float32)],
        compiler_params=_cparams(("parallel",)),
        name="attn_sample",
    )(sinks.reshape(1, N_Q_HEADS), zs, cache_k, cache_v, cos_t, sin_t)


def _log_sigmoid(x):
    return jnp.minimum(x, 0.0) - jnp.log(1.0 + jnp.exp(-jnp.abs(x)))


def _mlstm_chunk(q, k, v, li_col, lf_col, li_row, lf_row, c_state, n_state, m_state):
    t = q.shape[0]
    r = lax.broadcasted_iota(jnp.int32, (t, t), 0)
    c = lax.broadcasted_iota(jnp.int32, (t, t), 1)
    tril = c <= r
    b_col = jnp.sum(jnp.where(tril, lf_row, 0.0), axis=1, keepdims=True)
    b_row = jnp.sum(jnp.where(r <= c, lf_col, 0.0), axis=0, keepdims=True)
    b_last = jnp.sum(lf_row, axis=1, keepdims=True)
    log_d = jnp.where(tril, b_col - b_row + li_row, -jnp.inf)
    inter = b_col + m_state
    m_row = jnp.maximum(inter, jnp.max(log_d, axis=1, keepdims=True))
    qb = q.astype(jnp.bfloat16)
    kb = k.astype(jnp.bfloat16)
    vb = v.astype(jnp.bfloat16)
    qk = lax.dot_general(qb, kb, (((1,), (1,)), ((), ())), preferred_element_type=jnp.float32)
    s = qk * jnp.exp(log_d - m_row)
    w_inter = jnp.exp(inter - m_row)
    cq = lax.dot_general(qb, c_state.astype(jnp.bfloat16), (((1,), (1,)), ((), ())),
                         preferred_element_type=jnp.float32)
    num = jnp.dot(s.astype(jnp.bfloat16), vb, preferred_element_type=jnp.float32) + w_inter * cq
    den = jnp.sum(s, axis=1, keepdims=True) + w_inter * jnp.sum(q * n_state, axis=1, keepdims=True)
    h = num / jnp.maximum(jnp.abs(den), jnp.exp(-m_row))
    g_row = b_last - b_row + li_row
    g_col = b_last - b_col + li_col
    m_new = jnp.maximum(b_last + m_state, jnp.max(g_row, axis=1, keepdims=True))
    decay = jnp.exp(b_last + m_state - m_new)
    wk_col = jnp.exp(g_col - m_new)
    wv_t = (wk_col * v).T.astype(jnp.bfloat16)
    c_new = decay * c_state + jnp.dot(wv_t, kb, preferred_element_type=jnp.float32)
    n_new = decay * n_state + jnp.sum(wk_col * k, axis=0, keepdims=True)
    return h, c_new, n_new, m_new


def _mlstm_kernel(bias_ref, z_ref, og_ref, zg_ref, zgt_ref, gm_ref, *rest, n_valid, has_init):
    if has_init:
        c0_ref, n0_ref, m0_ref, o_ref, co_ref, no_ref, mo_ref, c_s, n_s, m_s = rest
    else:
        o_ref, co_ref, no_ref, mo_ref, c_s, n_s, m_s = rest
    t = MLSTM_T
    step = pl.program_id(1)
    last = pl.num_programs(1) - 1

    @pl.when(step == 0)
    def _():
        if has_init:
            c_s[...] = c0_ref[0]
            n_s[...] = n0_ref[0]
            m_s[...] = m0_ref[0]
        else:
            c_s[...] = jnp.zeros_like(c_s)
            n_s[...] = jnp.zeros_like(n_s)
            m_s[...] = jnp.zeros_like(m_s)

    load2d = lambda ref: ref[0] if len(ref.shape) == 3 else ref[...]
    z, og, zg, zgt = load2d(z_ref), load2d(og_ref), load2d(zg_ref), load2d(zgt_ref)
    rows = z.shape[0]
    if rows < t:
        pad = lambda a: jnp.concatenate([a, jnp.zeros((t - rows, a.shape[1]), a.dtype)], axis=0)
        z, og, zg = pad(z), pad(og), pad(zg)
    valid_col = lax.broadcasted_iota(jnp.int32, (t, 1), 0) < n_valid
    valid_row = lax.broadcasted_iota(jnp.int32, (1, t), 1) < n_valid
    for h in range(N_MLSTM_HEADS):
        q = z[:, COL_QM + h * MLSTM_DK:COL_QM + (h + 1) * MLSTM_DK]
        k = z[:, COL_KM + h * MLSTM_DK:COL_KM + (h + 1) * MLSTM_DK] * (MLSTM_DK ** -0.5)
        v = z[:, COL_VM + h * MLSTM_DV:COL_VM + (h + 1) * MLSTM_DV]
        b_i = bias_ref[0, h]
        b_f = bias_ref[1, h]
        li_col = zg[:, h:h + 1] + b_i
        lf_col = _log_sigmoid(zg[:, N_MLSTM_HEADS + h:N_MLSTM_HEADS + h + 1] + b_f)
        li_row = zgt[h:h + 1, :] + b_i
        lf_row = _log_sigmoid(zgt[N_MLSTM_HEADS + h:N_MLSTM_HEADS + h + 1, :] + b_f)
        if n_valid < t:
            li_col = jnp.where(valid_col, li_col, -jnp.inf)
            lf_col = jnp.where(valid_col, lf_col, 0.0)
            li_row = jnp.where(valid_row, li_row, -jnp.inf)
            lf_row = jnp.where(valid_row, lf_row, 0.0)
        hm, c_new, n_new, m_new = _mlstm_chunk(
            q, k, v, li_col, lf_col, li_row, lf_row, c_s[h], n_s[h:h + 1, :], m_s[h:h + 1, 0:1])
        c_s[h] = c_new
        n_s[h:h + 1, :] = n_new
        m_s[h:h + 1, :] = jnp.broadcast_to(m_new, (1, LANES))
        gate = jax.nn.sigmoid(og[:, h * MLSTM_DV:(h + 1) * MLSTM_DV])
        out = _rms(hm, gm_ref[h:h + 1, :]) * gate
        out = out[:rows].astype(o_ref.dtype)
        if len(o_ref.shape) == 3:
            o_ref[0, :, h * MLSTM_DV:(h + 1) * MLSTM_DV] = out
        else:
            o_ref[:, h * MLSTM_DV:(h + 1) * MLSTM_DV] = out

    @pl.when(step == last)
    def _():
        co_ref[0] = c_s[...]
        no_ref[0] = n_s[...]
        mo_ref[0] = m_s[...]


_MLSTM_SCRATCH = [pltpu.VMEM((N_MLSTM_HEADS, MLSTM_DV, MLSTM_DK), jnp.float32),
                  pltpu.VMEM((N_MLSTM_HEADS, MLSTM_DK), jnp.float32),
                  pltpu.VMEM((N_MLSTM_HEADS, LANES), jnp.float32)]


def _mlstm_state_shapes(n):
    return [jax.ShapeDtypeStruct((n, N_MLSTM_HEADS, MLSTM_DV, MLSTM_DK), jnp.float32),
            jax.ShapeDtypeStruct((n, N_MLSTM_HEADS, MLSTM_DK), jnp.float32),
            jax.ShapeDtypeStruct((n, N_MLSTM_HEADS, LANES), jnp.float32)]


def _mlstm_state_specs(index):
    return [pl.BlockSpec((1, N_MLSTM_HEADS, MLSTM_DV, MLSTM_DK), lambda *a: (index(*a), 0, 0, 0)),
            pl.BlockSpec((1, N_MLSTM_HEADS, MLSTM_DK), lambda *a: (index(*a), 0, 0)),
            pl.BlockSpec((1, N_MLSTM_HEADS, LANES), lambda *a: (index(*a), 0, 0))]


def _mlstm_prompt(z1, ztail, zg, zgt, bias, g_mlstm, batch, seq):
    t = MLSTM_T
    nc = seq // t
    row = lambda w: pl.BlockSpec((t, w), lambda n, c: (n * nc + c, 0))
    return pl.pallas_call(
        functools.partial(_mlstm_kernel, n_valid=t, has_init=False),
        grid=(batch, nc),
        in_specs=[pl.BlockSpec(memory_space=pltpu.SMEM),
                  row(HEAD_COLS), row(MLSTM_WIDTH), row(LANES),
                  pl.BlockSpec((8, t), lambda n, c: (0, n * nc + c)),
                  pl.BlockSpec((N_MLSTM_HEADS, MLSTM_DV), lambda n, c: (0, 0))],
        out_specs=[row(MLSTM_WIDTH)] + _mlstm_state_specs(lambda n, c: n),
        out_shape=[jax.ShapeDtypeStruct((batch * seq, MLSTM_WIDTH), jnp.bfloat16)] + _mlstm_state_shapes(batch),
        scratch_shapes=_MLSTM_SCRATCH,
        compiler_params=_cparams(("parallel", "arbitrary")),
        name="mlstm_prompt",
    )(bias, z1, ztail, zg, zgt, g_mlstm)


def _mlstm_sample(zs, zts, zgs, zgts, bias, g_mlstm, c0, n0, m0, n_valid):
    s, rows, _ = zs.shape
    seq_blk = lambda r, c: pl.BlockSpec((1, r, c), lambda n, _: (n, 0, 0))
    return pl.pallas_call(
        functools.partial(_mlstm_kernel, n_valid=n_valid, has_init=True),
        grid=(s, 1),
        in_specs=[pl.BlockSpec(memory_space=pltpu.SMEM),
                  seq_blk(rows, HEAD_COLS), seq_blk(rows, MLSTM_WIDTH), seq_blk(rows, LANES),
                  seq_blk(8, MLSTM_T),
                  pl.BlockSpec((N_MLSTM_HEADS, MLSTM_DV), lambda n, _: (0, 0))]
                 + _mlstm_state_specs(lambda n, _: n),
        out_specs=[seq_blk(rows, MLSTM_WIDTH)] + _mlstm_state_specs(lambda n, _: n),
        out_shape=[jax.ShapeDtypeStruct((s, rows, MLSTM_WIDTH), jnp.float32)] + _mlstm_state_shapes(s),
        scratch_shapes=_MLSTM_SCRATCH,
        compiler_params=_cparams(("parallel", "arbitrary")),
        name="mlstm_sample",
    )(bias, zs, zts, zgs, zgts, g_mlstm, c0, n0, m0)


def _conv_norm_act(cv, cb, g, b):
    cv = cv + cb
    mu = jnp.mean(cv, axis=-1, keepdims=True)
    d = cv - mu
    var = jnp.mean(d * d, axis=-1, keepdims=True)
    y = d * lax.rsqrt(var + NORM_EPS) * g + b
    return y * jax.nn.sigmoid(y)


def _conv_prompt_kernel(ga_ref, gb_ref, w_ref, cb_ref, g_ref, b_ref, o_ref, st_ref, ext_ref, cv_ref, *, nt):
    tb = CONV_TB
    i = pl.program_id(1)

    @pl.when(i == 0)
    def _():
        ext_ref[0:CONV_HIST, :] = jnp.zeros((CONV_HIST, CONV_CH), jnp.float32)

    ext_ref[CONV_HIST:, :] = ga_ref[...] * jax.nn.sigmoid(gb_ref[...])
    off = CONV_HIST - (CONV_WIDTH - 1)
    rb = 8
    for lc in range(CONV_CH // LANES):
        cols = slice(lc * LANES, (lc + 1) * LANES)
        wb = [jnp.broadcast_to(w_ref[j:j + 1, cols], (rb, LANES)) for j in range(CONV_WIDTH)]
        for r in range(tb // rb):
            acc = wb[0] * ext_ref[r * rb + off:r * rb + off + rb, cols]
            for j in range(1, CONV_WIDTH):
                acc = acc + wb[j] * ext_ref[r * rb + off + j:r * rb + off + j + rb, cols]
            cv_ref[r * rb:(r + 1) * rb, cols] = acc
    o_ref[...] = _conv_norm_act(cv_ref[...], cb_ref[...], g_ref[...], b_ref[...]).astype(o_ref.dtype)

    @pl.when(i == nt - 1)
    def _():
        st_ref[0] = ext_ref[tb + off:tb + CONV_HIST, :]

    ext_ref[0:CONV_HIST, :] = ext_ref[tb:tb + CONV_HIST, :]


def _conv_prompt(ztail, conv_w, conv_b, g_conv, b_conv, batch, seq):
    tb = CONV_TB
    nt = seq // tb
    vec = pl.BlockSpec((1, CONV_CH), lambda n, i: (0, 0))
    row = lambda col: pl.BlockSpec((tb, CONV_CH), lambda n, i: (n * nt + i, col))
    return pl.pallas_call(
        functools.partial(_conv_prompt_kernel, nt=nt),
        grid=(batch, nt),
        in_specs=[row(MLSTM_WIDTH // CONV_CH), row(MLSTM_WIDTH // CONV_CH + 1),
                  pl.BlockSpec((CONV_WIDTH, CONV_CH), lambda n, i: (0, 0)), vec, vec, vec],
        out_specs=[row(0), pl.BlockSpec((1, CONV_WIDTH - 1, CONV_CH), lambda n, i: (n, 0, 0))],
        out_shape=[jax.ShapeDtypeStruct((batch * seq, CONV_CH), jnp.bfloat16),
                   jax.ShapeDtypeStruct((batch, CONV_WIDTH - 1, CONV_CH), jnp.float32)],
        scratch_shapes=[pltpu.VMEM((tb + CONV_HIST, CONV_CH), jnp.float32),
                        pltpu.VMEM((tb, CONV_CH), jnp.float32)],
        compiler_params=_cparams(("parallel", "arbitrary")),
        name="conv_prompt",
    )(ztail, ztail, conv_w, conv_b.reshape(1, -1), g_conv.reshape(1, -1), b_conv.reshape(1, -1))


def _conv_sample_kernel(ga_ref, gb_ref, st_ref, w_ref, cb_ref, g_ref, b_ref, o_ref, sto_ref, ext_ref, *, n_new):
    hist = CONV_WIDTH - 1
    ext_ref[0:hist] = st_ref[...]
    ext_ref[hist:] = ga_ref[...] * jax.nn.sigmoid(gb_ref[...])
    for t in range(n_new):
        acc = w_ref[0:1, :] * ext_ref[t]
        for j in range(1, CONV_WIDTH):
            acc = acc + w_ref[j:j + 1, :] * ext_ref[t + j]
        o_ref[t] = _conv_norm_act(acc, cb_ref[...], g_ref[...], b_ref[...])
    sto_ref[...] = ext_ref[n_new:]


def _conv_sample(ga, gb, state, conv_w, conv_b, g_conv, b_conv):
    n_new, s, ch = ga.shape
    hist = CONV_WIDTH - 1
    full = lambda shape: pl.BlockSpec(shape, lambda i: (0,) * len(shape))
    return pl.pallas_call(
        functools.partial(_conv_sample_kernel, n_new=n_new),
        grid=(1,),
        in_specs=[full((n_new, s, ch)), full((n_new, s, ch)), full((hist, s, ch)),
                  full((CONV_WIDTH, ch)), full((1, ch)), full((1, ch)), full((1, ch))],
        out_specs=[full((n_new, s, ch)), full((hist, s, ch))],
        out_shape=[jax.ShapeDtypeStruct((n_new, s, ch), jnp.float32),
                   jax.ShapeDtypeStruct((hist, s, ch), jnp.float32)],
        scratch_shapes=[pltpu.VMEM((hist + n_new, s, ch), jnp.float32)],
        compiler_params=_cparams(("arbitrary",)),
        name="conv_sample",
    )(ga, gb, state, conv_w, conv_b.reshape(1, -1), g_conv.reshape(1, -1), b_conv.reshape(1, -1))


def _layer(x, h, batch, seq, s_batch, s_len, ck, cv, c0, n0, m0, conv_state, p, g_next):
    (w_in, sinks, b_ig, b_fg, g_mlstm, conv_w, conv_b, g_conv, b_conv,
     w_out, g_post_mix, g_pre_mlp, w_up, w_down, g_post_mlp) = p
    mp = batch * seq
    rows = 8
    assert s_len <= rows

    z1 = _matmul(h, w_in, n_cols=HEAD_COLS, tn=HEAD_COLS // 6, out_dtype=jnp.float32, name="proj_in_head")
    zg = _matmul(h, w_in, n_cols=LANES, col_block0=COL_GATE // LANES, tn=LANES,
                 out_dtype=jnp.float32, name="proj_in_gate")
    w_tail = w_in[:, COL_OG:]
    zt = _matmul(h, w_tail, n_cols=TAIL_COLS, tn=1024, out_dtype=jnp.float32, name="proj_in_tail")
    zgt = zg[:, :8].T
    bias = jnp.stack([b_ig, b_fg])

    def pad_rows(a):
        a = a.reshape(s_batch, s_len, a.shape[-1])
        return jnp.pad(a, ((0, 0), (0, rows - s_len), (0, 0)))

    zs, zts, zgs = pad_rows(z1[mp:]), pad_rows(zt[mp:]), pad_rows(zg[mp:])
    zgts = jnp.pad(zg[mp:, :8].reshape(s_batch, s_len, 8).transpose(0, 2, 1),
                   ((0, 0), (0, 0), (0, MLSTM_T - s_len)))

    a_p, kwin_p, vwin_p = _attn_prompt(z1, sinks, batch, seq)
    a_s, kwin_s, vwin_s = _attn_sample(zs, ck.reshape(s_batch, WINDOW, KV_WIDTH),
                                       cv.reshape(s_batch, WINDOW, KV_WIDTH), sinks, s_len)
    b_p, c_p, n_p, m_p = _mlstm_prompt(z1, zt, zg, zgt, bias, g_mlstm, batch, seq)
    m0b = jnp.broadcast_to(m0[:, :, None], m0.shape + (LANES,))
    b_s, c_s, n_s, m_s = _mlstm_sample(zs, zts, zgs, zgts, bias, g_mlstm, c0, n0, m0b, s_len)
    c_pr, st_p = _conv_prompt(zt, conv_w, conv_b, g_conv, b_conv, batch, seq)
    tm = lambda a: a.reshape(s_batch, s_len, CONV_CH).transpose(1, 0, 2)
    ga_s = tm(zt[mp:, MLSTM_WIDTH:MLSTM_WIDTH + CONV_CH])
    gb_s = tm(zt[mp:, MLSTM_WIDTH + CONV_CH:])
    c_sm, st_s = _conv_sample(ga_s, gb_s, conv_state.transpose(1, 0, 2), conv_w, conv_b, g_conv, b_conv)

    unpad = lambda a: a[:, :s_len].reshape(s_batch * s_len, a.shape[-1]).astype(jnp.bfloat16)
    mix_s = jnp.concatenate(
        [unpad(a_s), unpad(b_s), c_sm.transpose(1, 0, 2).reshape(s_batch * s_len, CONV_CH).astype(jnp.bfloat16)],
        axis=-1)
    mix = jnp.concatenate([jnp.concatenate([a_p, b_p, c_pr], axis=-1), mix_s], axis=0)

    y = _matmul(mix, w_out, n_cols=D_MODEL, tn=1024, out_dtype=jnp.float32, name="proj_out")
    x, h2 = _resid(y, x, g_post_mix, g_pre_mlp)
    ff = _matmul(h2, w_up, n_cols=w_up.shape[1], tn=1024, out_dtype=jnp.bfloat16, relu2=True, name="mlp_up")
    y2 = _matmul(ff, w_down, n_cols=D_MODEL, tn=1024, out_dtype=jnp.float32, name="mlp_down")
    x, h_next = _resid(y2, x, g_post_mlp, g_next)

    kv = lambda a, n: a.reshape(n, WINDOW, N_KV_HEADS, HEAD_DIM)
    prompt_state = (kv(kwin_p, batch), kv(vwin_p, batch), c_p, n_p, m_p[:, :, 0], st_p)
    sample_state = (kv(kwin_s, s_batch), kv(vwin_s, s_batch), c_s, n_s, m_s[:, :, 0], st_s.transpose(1, 0, 2))
    return x, h_next, prompt_state, sample_state


def kernel(x_prompt, x_sample, cache_win_k, cache_win_v, state_mlstm_C, state_mlstm_n, state_mlstm_m, state_conv,
           g_pre_mix, w_in, attn_sinks, b_igate, b_fgate, g_mlstm, conv_w, conv_b, g_conv, b_conv,
           w_out, g_post_mix, g_pre_mlp, w_up, w_down, g_post_mlp):
    batch, seq, d = x_prompt.shape
    s_batch, s_len, _ = x_sample.shape
    depth = w_in.shape[0]
    mp = batch * seq
    x = jnp.concatenate([x_prompt.reshape(mp, d), x_sample.reshape(s_batch * s_len, d)], axis=0)
    h = _norm(x, g_pre_mix[0])
    p_states, s_states = [], []
    for l in range(depth):
        params = (w_in[l], attn_sinks[l], b_igate[l], b_fgate[l], g_mlstm[l], conv_w[l], conv_b[l],
                  g_conv[l], b_conv[l], w_out[l], g_post_mix[l], g_pre_mlp[l], w_up[l], w_down[l], g_post_mlp[l])
        g_next = g_pre_mix[l + 1] if l + 1 < depth else None
        x, h, ps, ss = _layer(x, h, batch, seq, s_batch, s_len, cache_win_k[l], cache_win_v[l],
                              state_mlstm_C[l], state_mlstm_n[l], state_mlstm_m[l], state_conv[l], params, g_next)
        p_states.append(ps)
        s_states.append(ss)
    stack = lambda states, i: jnp.stack([st[i] for st in states])
    return ((x[:mp].reshape(batch, seq, d), x[mp:].reshape(s_batch, s_len, d))
            + tuple(stack(p_states, i) for i in range(6))
            + tuple(stack(s_states, i) for i in range(6)))
```

```python
import functools
import math

import jax
import jax.numpy as jnp
from jax import lax
from jax.experimental import pallas as pl
from jax.experimental.pallas import tpu as pltpu

D_MODEL = 4096
HEAD_DIM = 64
N_Q_HEADS = 16
N_KV_HEADS = 2
Q_PER_KV = N_Q_HEADS // N_KV_HEADS
WINDOW = 128
ROPE_THETA = 10000.0
N_MLSTM_HEADS = 4
MLSTM_DV = 512
MLSTM_DK = 256
CONV_CH = 1024
CONV_WIDTH = 31
NORM_EPS = 1e-6
PAST_LEN = 16384

ATT_WIDTH = N_Q_HEADS * HEAD_DIM
KV_WIDTH = N_KV_HEADS * HEAD_DIM
MLSTM_WIDTH = N_MLSTM_HEADS * MLSTM_DV
COL_QA = 0
COL_KA = COL_QA + ATT_WIDTH
COL_VA = COL_KA + KV_WIDTH
COL_QM = COL_VA + KV_WIDTH
COL_KM = COL_QM + N_MLSTM_HEADS * MLSTM_DK
COL_VM = COL_KM + N_MLSTM_HEADS * MLSTM_DK
COL_GATE = COL_VM + MLSTM_WIDTH
COL_OG = COL_GATE + 2 * N_MLSTM_HEADS
HEAD_COLS = COL_GATE
TAIL_COLS = MLSTM_WIDTH + 2 * CONV_CH

LANES = 128
SUBLANES = 8
MXU_COLS = 256
MLSTM_T = 128
CONV_TB = 256
CONV_RSUB = 64
CONV_HIST = 32
VMEM_LIMIT = 56 * 1024 * 1024


def _largest_tile(total, target, multiple):
    best = None
    for t in range(multiple, min(total, target) + 1, multiple):
        if total % t == 0:
            best = t
    assert best is not None, (total, target, multiple)
    return best


def _cparams(sem):
    return pltpu.CompilerParams(dimension_semantics=sem, vmem_limit_bytes=VMEM_LIMIT)


def _rms(x, g):
    return x * lax.rsqrt(jnp.mean(x * x, axis=-1, keepdims=True) + NORM_EPS) * g


def _row_tile(mp, ms):
    return _largest_tile(math.gcd(mp, ms), 256, 16)


def _split_specs(tr, d, nbp):
    prompt = pl.BlockSpec((tr, d), lambda i: (jnp.minimum(i, nbp - 1), 0))
    sample = pl.BlockSpec((tr, d), lambda i: (jnp.maximum(i - nbp, 0), 0))
    return prompt, sample


def _norm_kernel(xp_ref, xs_ref, g_ref, h_ref, *, nbp):
    i = pl.program_id(0)

    @pl.when(i < nbp)
    def _():
        h_ref[...] = _rms(xp_ref[...], g_ref[...]).astype(h_ref.dtype)

    @pl.when(i >= nbp)
    def _():
        h_ref[...] = _rms(xs_ref[...], g_ref[...]).astype(h_ref.dtype)


def _norm(xp, xs, g):
    (mp, d), ms = xp.shape, xs.shape[0]
    tr = _row_tile(mp, ms)
    nbp = mp // tr
    prompt, sample = _split_specs(tr, d, nbp)
    return pl.pallas_call(
        functools.partial(_norm_kernel, nbp=nbp),
        grid=((mp + ms) // tr,),
        in_specs=[prompt, sample, pl.BlockSpec((1, d), lambda i: (0, 0))],
        out_specs=pl.BlockSpec((tr, d), lambda i: (i, 0)),
        out_shape=jax.ShapeDtypeStruct((mp + ms, d), jnp.bfloat16),
        compiler_params=_cparams(("arbitrary",)),
        name="rmsnorm",
    )(xp, xs, g.reshape(1, d))


def _resid_kernel(y_ref, xp_ref, xs_ref, gpost_ref, *rest, nbp, with_next):
    if with_next:
        gnext_ref, xop_ref, xos_ref, h_ref = rest
    else:
        xop_ref, xos_ref = rest
    i = pl.program_id(0)

    def update(x_ref, xo_ref):
        xn = x_ref[...] + _rms(y_ref[...], gpost_ref[...])
        xo_ref[...] = xn
        if with_next:
            h_ref[...] = _rms(xn, gnext_ref[...]).astype(h_ref.dtype)

    pl.when(i < nbp)(lambda: update(xp_ref, xop_ref))
    pl.when(i >= nbp)(lambda: update(xs_ref, xos_ref))


def _resid(y, xp, xs, g_post, g_next):
    (mp, d), ms = xp.shape, xs.shape[0]
    tr = _row_tile(mp, ms)
    nbp = mp // tr
    prompt, sample = _split_specs(tr, d, nbp)
    row = pl.BlockSpec((tr, d), lambda i: (i, 0))
    vec = pl.BlockSpec((1, d), lambda i: (0, 0))
    with_next = g_next is not None
    out_specs = [prompt, sample] + ([row] if with_next else [])
    out_shape = [jax.ShapeDtypeStruct((mp, d), jnp.float32), jax.ShapeDtypeStruct((ms, d), jnp.float32)]
    args = [y, xp, xs, g_post.reshape(1, d)]
    if with_next:
        out_shape.append(jax.ShapeDtypeStruct((mp + ms, d), jnp.bfloat16))
        args.append(g_next.reshape(1, d))
    out = pl.pallas_call(
        functools.partial(_resid_kernel, nbp=nbp, with_next=with_next),
        grid=((mp + ms) // tr,),
        in_specs=[row, prompt, sample, vec] + ([vec] if with_next else []),
        out_specs=out_specs, out_shape=out_shape,
        compiler_params=_cparams(("arbitrary",)),
        name="resid_norm" if with_next else "resid_last",
    )(*args)
    return (out[0], out[1], out[2]) if with_next else (out[0], out[1], None)


def _mm_kernel(*refs, n_a, k_bounds, nk, relu2):
    a_refs, b_ref, o_ref, scratch = refs[:n_a], refs[n_a], refs[n_a + 1], refs[n_a + 2:]
    acc_ref = scratch[0] if scratch else o_ref
    k = pl.program_id(2)

    for t, a_ref in enumerate(a_refs):
        lo, hi = k_bounds[t], k_bounds[t + 1]

        def part(a_ref=a_ref):
            return jnp.dot(a_ref[...], b_ref[...].astype(jnp.bfloat16), preferred_element_type=jnp.float32)

        def assign(part=part):
            acc_ref[...] = part()

        def accumulate(part=part):
            acc_ref[...] += part()

        if lo == 0:
            pl.when(k == 0)(assign)
            if hi > 1:
                pl.when((k > 0) & (k < hi))(accumulate)
        else:
            pl.when((k >= lo) & (k < hi))(accumulate)

    if scratch or relu2:
        @pl.when(k == nk - 1)
        def _():
            r = acc_ref[...]
            if relu2:
                r = jnp.square(jnp.maximum(r, 0.0))
            o_ref[...] = r.astype(o_ref.dtype)


def _matmul(a_list, b, layer, *, n_cols, col_block0=0, tn, out_dtype, relu2=False, name):
    m = a_list[0].shape[0]
    tm = _largest_tile(m, 2080, 16)
    tk = _largest_tile(math.gcd(*[a.shape[1] for a in a_list]), 1024, LANES)
    k_bounds = [0]
    for a in a_list:
        k_bounds.append(k_bounds[-1] + a.shape[1] // tk)
    nk = k_bounds[-1]
    assert nk * tk == b.shape[1] and n_cols % tn == 0

    def a_spec(lo, hi):
        return pl.BlockSpec((tm, tk), lambda i, j, k: (i, jnp.clip(k - lo, 0, hi - lo - 1)))

    scratch = [] if out_dtype == jnp.float32 else [pltpu.VMEM((tm, tn), jnp.float32)]
    return pl.pallas_call(
        functools.partial(_mm_kernel, n_a=len(a_list), k_bounds=tuple(k_bounds), nk=nk, relu2=relu2),
        grid=(m // tm, n_cols // tn, nk),
        in_specs=[a_spec(k_bounds[t], k_bounds[t + 1]) for t in range(len(a_list))]
                 + [pl.BlockSpec((None, tk, tn), lambda i, j, k: (layer, k, j + col_block0))],
        out_specs=pl.BlockSpec((tm, tn), lambda i, j, k: (i, j)),
        out_shape=jax.ShapeDtypeStruct((m, n_cols), out_dtype),
        scratch_shapes=scratch,
        compiler_params=_cparams(("parallel", "parallel", "arbitrary")),
        name=name,
    )(*a_list, b)


def _mm_fullk_kernel(*refs, n_a, relu2):
    a_refs, b_ref, o_ref = refs[:n_a], refs[n_a], refs[n_a + 1]
    acc, lo = None, 0
    for a_ref in a_refs:
        width = a_ref.shape[1]
        part = jnp.dot(a_ref[...], b_ref[lo:lo + width, :].astype(jnp.bfloat16),
                       preferred_element_type=jnp.float32)
        acc = part if acc is None else acc + part
        lo += width
    if relu2:
        acc = jnp.square(jnp.maximum(acc, 0.0))
    o_ref[...] = acc.astype(o_ref.dtype)


def _matmul_fullk(a_list, b, layer, *, n_cols, col_block0=0, tn, out_dtype, relu2=False, tm_target=2080, name):
    m = a_list[0].shape[0]
    kdim = sum(a.shape[1] for a in a_list)
    tm = _largest_tile(m, tm_target, 16)
    assert kdim == b.shape[1] and n_cols % tn == 0
    return pl.pallas_call(
        functools.partial(_mm_fullk_kernel, n_a=len(a_list), relu2=relu2),
        grid=(m // tm, n_cols // tn),
        in_specs=[pl.BlockSpec((tm, a.shape[1]), lambda i, j: (i, 0), pipeline_mode=pl.Buffered(1))
                  for a in a_list]
                 + [pl.BlockSpec((None, kdim, tn), lambda i, j: (layer, 0, j + col_block0))],
        out_specs=pl.BlockSpec((tm, tn), lambda i, j: (i, j)),
        out_shape=jax.ShapeDtypeStruct((m, n_cols), out_dtype),
        compiler_params=_cparams(("parallel", "arbitrary")),
        name=name,
    )(*a_list, b)


def _rope_tables(pos0, length):
    half = HEAD_DIM // 2
    inv_freq = ROPE_THETA ** (-jnp.arange(half, dtype=jnp.float32) / half)
    pos = (pos0 + jnp.arange(length, dtype=jnp.int32)).astype(jnp.float32)
    ang = pos[:, None] * inv_freq[None, :]
    cos, sin = jnp.cos(ang), jnp.sin(ang)
    reps = LANES // HEAD_DIM
    cos_t = jnp.tile(jnp.concatenate([cos, cos], axis=-1), (1, reps))
    sin_t = jnp.tile(jnp.concatenate([-sin, sin], axis=-1), (1, reps))
    return cos_t, sin_t


def _rope(x, cos_t, sin_t):
    half = HEAD_DIM // 2
    lane = lax.broadcasted_iota(jnp.int32, x.shape, 1)
    swapped = jnp.where((lane % HEAD_DIM) < half,
                        pltpu.roll(x, LANES - half, 1), pltpu.roll(x, half, 1))
    return x * cos_t + swapped * sin_t


def _rope_wide(x, cos_t, sin_t):
    return jnp.concatenate(
        [_rope(x[:, c * LANES:(c + 1) * LANES], cos_t, sin_t) for c in range(x.shape[1] // LANES)], axis=-1)


def _attend(q_rot, k_ext, v_ext, mask_fn, sink_ref):
    rq, s_len = q_rot.shape[0], k_ext.shape[0]
    kb = k_ext.astype(jnp.bfloat16)
    vb = v_ext.astype(jnp.bfloat16)
    scale = HEAD_DIM ** -0.5
    rows = lax.broadcasted_iota(jnp.int32, (Q_PER_KV * rq, s_len), 0)
    mask = mask_fn(rows % rq, lax.broadcasted_iota(jnp.int32, (Q_PER_KV * rq, s_len), 1))
    outs = []
    for g in range(N_KV_HEADS):
        kg = kb[:, g * HEAD_DIM:(g + 1) * HEAD_DIM]
        vg = vb[:, g * HEAD_DIM:(g + 1) * HEAD_DIM]
        heads = range(g * Q_PER_KV, (g + 1) * Q_PER_KV)
        qg = jnp.concatenate([q_rot[:, h * HEAD_DIM:(h + 1) * HEAD_DIM] for h in heads], axis=0)
        sink = jnp.concatenate([jnp.full((rq, 1), sink_ref[0, h], jnp.float32) for h in heads], axis=0)
        s = lax.dot_general((qg * scale).astype(jnp.bfloat16), kg, (((1,), (1,)), ((), ())),
                            preferred_element_type=jnp.float32)
        s = jnp.where(mask, s, -jnp.inf)
        m = jnp.maximum(jnp.max(s, axis=-1, keepdims=True), sink)
        p = jnp.exp(s - m)
        denom = jnp.sum(p, axis=-1, keepdims=True) + jnp.exp(sink - m)
        o = jnp.dot(p.astype(jnp.bfloat16), vg, preferred_element_type=jnp.float32) / denom
        outs.extend(o[hh * rq:(hh + 1) * rq] for hh in range(Q_PER_KV))
    return jnp.concatenate(outs, axis=-1)


def _attn_prompt_kernel(sink_ref, q_ref, kc_ref, kp_ref, vc_ref, vp_ref,
                        cosc_ref, sinc_ref, cosp_ref, sinp_ref,
                        o_ref, kwin_ref, vwin_ref, *, nb):
    j = pl.program_id(1)
    cosc, sinc = cosc_ref[...], sinc_ref[...]
    q_rot = _rope_wide(q_ref[...], cosc, sinc)
    k_cur = _rope(kc_ref[...], cosc, sinc)
    k_prev = _rope(kp_ref[...], cosp_ref[...], sinp_ref[...])
    k_ext = jnp.concatenate([k_prev, k_cur], axis=0)
    v_ext = jnp.concatenate([vp_ref[...], vc_ref[...]], axis=0)

    def mask_fn(tq, sk):
        return ((sk >= tq) & (sk < WINDOW) & (j > 0)) | ((sk >= WINDOW) & ((sk - WINDOW) <= tq))

    o_ref[...] = _attend(q_rot, k_ext, v_ext, mask_fn, sink_ref).astype(o_ref.dtype)

    @pl.when(j == nb - 1)
    def _():
        kwin_ref[0] = k_cur
        vwin_ref[0] = vc_ref[...]


def _attn_prompt(z1, sinks, batch, seq):
    nb = seq // WINDOW
    cos_t, sin_t = _rope_tables(0, seq)
    blk = lambda col: pl.BlockSpec((WINDOW, LANES), lambda n, j: (n * nb + j, col))
    blk_prev = lambda col: pl.BlockSpec((WINDOW, LANES), lambda n, j: (n * nb + jnp.maximum(j - 1, 0), col))
    tab = pl.BlockSpec((WINDOW, LANES), lambda n, j: (j, 0))
    tab_prev = pl.BlockSpec((WINDOW, LANES), lambda n, j: (jnp.maximum(j - 1, 0), 0))
    win = pl.BlockSpec((1, WINDOW, KV_WIDTH), lambda n, j: (n, 0, 0))
    return pl.pallas_call(
        functools.partial(_attn_prompt_kernel, nb=nb),
        grid=(batch, nb),
        in_specs=[pl.BlockSpec(memory_space=pltpu.SMEM),
                  pl.BlockSpec((WINDOW, ATT_WIDTH), lambda n, j: (n * nb + j, 0)),
                  blk(COL_KA // LANES), blk_prev(COL_KA // LANES),
                  blk(COL_VA // LANES), blk_prev(COL_VA // LANES),
                  tab, tab, tab_prev, tab_prev],
        out_specs=[pl.BlockSpec((WINDOW, ATT_WIDTH), lambda n, j: (n * nb + j, 0)), win, win],
        out_shape=[jax.ShapeDtypeStruct((z1.shape[0], ATT_WIDTH), jnp.bfloat16),
                   jax.ShapeDtypeStruct((batch, WINDOW, KV_WIDTH), jnp.float32),
                   jax.ShapeDtypeStruct((batch, WINDOW, KV_WIDTH), jnp.float32)],
        compiler_params=_cparams(("parallel", "arbitrary")),
        name="attn_prompt",
    )(sinks.reshape(1, N_Q_HEADS), z1, z1, z1, z1, z1, cos_t, sin_t, cos_t, sin_t)


def _attn_sample_kernel(sink_ref, z_ref, ck_ref, cv_ref, cos_ref, sin_ref,
                        o_ref, kout_ref, vout_ref, kext_ref, vext_ref, *, n_new, rows):
    cos_t, sin_t = cos_ref[...], sin_ref[...]
    z = z_ref[0]
    q_rot = _rope_wide(z[:, COL_QA:COL_QA + ATT_WIDTH], cos_t, sin_t)
    k_new = _rope(z[:, COL_KA:COL_KA + KV_WIDTH], cos_t, sin_t)
    v_new = z[:, COL_VA:COL_VA + KV_WIDTH]
    kext_ref[0:WINDOW, :] = ck_ref[0]
    vext_ref[0:WINDOW, :] = cv_ref[0]
    kext_ref[WINDOW:WINDOW + rows, :] = k_new
    vext_ref[WINDOW:WINDOW + rows, :] = v_new
    fill = jnp.zeros((WINDOW - rows, KV_WIDTH), jnp.float32)
    kext_ref[WINDOW + rows:, :] = fill
    vext_ref[WINDOW + rows:, :] = fill

    def mask_fn(tq, sk):
        return ((sk >= tq) & (sk < WINDOW)) | ((sk >= WINDOW) & ((sk - WINDOW) <= tq) & (sk < WINDOW + n_new))

    o_ref[0] = _attend(q_rot, kext_ref[...], vext_ref[...], mask_fn, sink_ref)
    kout_ref[0] = kext_ref[pl.ds(n_new, WINDOW), :]
    vout_ref[0] = vext_ref[pl.ds(n_new, WINDOW), :]


def _attn_sample(zs, cache_k, cache_v, sinks, n_new):
    s, rows, _ = zs.shape
    cos_t, sin_t = _rope_tables(PAST_LEN, rows)
    seq_blk = lambda r, c: pl.BlockSpec((1, r, c), lambda n: (n, 0, 0))
    tab = pl.BlockSpec((rows, LANES), lambda n: (0, 0))
    return pl.pallas_call(
        functools.partial(_attn_sample_kernel, n_new=n_new, rows=rows),
        grid=(s,),
        in_specs=[pl.BlockSpec(memory_space=pltpu.SMEM),
                  seq_blk(rows, HEAD_COLS), seq_blk(WINDOW, KV_WIDTH), seq_blk(WINDOW, KV_WIDTH), tab, tab],
        out_specs=[seq_blk(rows, ATT_WIDTH), seq_blk(WINDOW, KV_WIDTH), seq_blk(WINDOW, KV_WIDTH)],
        out_shape=[jax.ShapeDtypeStruct((s, rows, ATT_WIDTH), jnp.float32),
                   jax.ShapeDtypeStruct((s, WINDOW, KV_WIDTH), jnp.float32),
                   jax.ShapeDtypeStruct((s, WINDOW, KV_WIDTH), jnp.float32)],
        scratch_shapes=[pltpu.VMEM((2 * WINDOW, KV_WIDTH), jnp.float32),
                        pltpu.VMEM((2 * WINDOW, KV_WIDTH), jnp.float32)],
        compiler_params=_cparams(("parallel",)),
        name="attn_sample",
    )(sinks.reshape(1, N_Q_HEADS), zs, cache_k, cache_v, cos_t, sin_t)


def _log_sigmoid(x):
    return jnp.minimum(x, 0.0) - jnp.log(1.0 + jnp.exp(-jnp.abs(x)))


def _mlstm_chunk(q, k, v, li_col, lf_col, li_row, lf_row, c_state, n_state, m_state):
    t = q.shape[0]
    r = lax.broadcasted_iota(jnp.int32, (t, t), 0)
    c = lax.broadcasted_iota(jnp.int32, (t, t), 1)
    tril = c <= r
    b_col = jnp.sum(jnp.where(tril, lf_row, 0.0), axis=1, keepdims=True)
    b_row = jnp.sum(jnp.where(r <= c, lf_col, 0.0), axis=0, keepdims=True)
    b_last = jnp.sum(lf_row, axis=1, keepdims=True)
    log_d = jnp.where(tril, b_col - b_row + li_row, -jnp.inf)
    inter = b_col + m_state
    m_row = jnp.maximum(inter, jnp.max(log_d, axis=1, keepdims=True))
    qb = q.astype(jnp.bfloat16)
    kb = k.astype(jnp.bfloat16)
    vb = v.astype(jnp.bfloat16)
    qk = lax.dot_general(qb, kb, (((1,), (1,)), ((), ())), preferred_element_type=jnp.float32)
    s = qk * jnp.exp(log_d - m_row)
    w_inter = jnp.exp(inter - m_row)
    cq = lax.dot_general(qb, c_state.astype(jnp.bfloat16), (((1,), (1,)), ((), ())),
                         preferred_element_type=jnp.float32)
    num = jnp.dot(s.astype(jnp.bfloat16), vb, preferred_element_type=jnp.float32) + w_inter * cq
    den = jnp.sum(s, axis=1, keepdims=True) + w_inter * jnp.sum(q * n_state, axis=1, keepdims=True)
    h = num / jnp.maximum(jnp.abs(den), jnp.exp(-m_row))
    g_row = b_last - b_row + li_row
    g_col = b_last - b_col + li_col
    m_new = jnp.maximum(b_last + m_state, jnp.max(g_row, axis=1, keepdims=True))
    decay = jnp.exp(b_last + m_state - m_new)
    wk_col = jnp.exp(g_col - m_new)
    wv_t = (wk_col * v).T.astype(jnp.bfloat16)
    c_new = decay * c_state + jnp.dot(wv_t, kb, preferred_element_type=jnp.float32)
    n_new = decay * n_state + jnp.sum(wk_col * k, axis=0, keepdims=True)
    return h, c_new, n_new, m_new


def _mlstm_kernel(bias_ref, z_ref, og_ref, zg_ref, zgt_ref, gm_ref, *rest, n_valid, has_init, has_prev=False):
    if has_init:
        c0_ref, n0_ref, m0_ref = rest[:3]
        o_ref, co_ref, no_ref, mo_ref, c_s, n_s, m_s = rest[4:] if has_prev else rest[3:]
    else:
        o_ref, co_ref, no_ref, mo_ref, c_s, n_s, m_s = rest
    t = MLSTM_T
    step = pl.program_id(1)
    last = pl.num_programs(1) - 1

    @pl.when(step == 0)
    def _():
        if has_init:
            c_s[...] = c0_ref[0]
            n_s[...] = n0_ref[0]
            m_s[...] = m0_ref[0]
        else:
            c_s[...] = jnp.zeros_like(c_s)
            n_s[...] = jnp.zeros_like(n_s)
            m_s[...] = jnp.zeros_like(m_s)

    load2d = lambda ref: ref[0] if len(ref.shape) == 3 else ref[...]
    z, og, zg, zgt = load2d(z_ref), load2d(og_ref), load2d(zg_ref), load2d(zgt_ref)
    rows = z.shape[0]
    if rows < t:
        pad = lambda a: jnp.concatenate([a, jnp.zeros((t - rows, a.shape[1]), a.dtype)], axis=0)
        z, og, zg = pad(z), pad(og), pad(zg)
    valid_col = lax.broadcasted_iota(jnp.int32, (t, 1), 0) < n_valid
    valid_row = lax.broadcasted_iota(jnp.int32, (1, t), 1) < n_valid
    for h in range(N_MLSTM_HEADS):
        q = z[:, COL_QM + h * MLSTM_DK:COL_QM + (h + 1) * MLSTM_DK]
        k = z[:, COL_KM + h * MLSTM_DK:COL_KM + (h + 1) * MLSTM_DK] * (MLSTM_DK ** -0.5)
        v = z[:, COL_VM + h * MLSTM_DV:COL_VM + (h + 1) * MLSTM_DV]
        b_i = bias_ref[0, h]
        b_f = bias_ref[1, h]
        li_col = zg[:, h:h + 1] + b_i
        lf_col = _log_sigmoid(zg[:, N_MLSTM_HEADS + h:N_MLSTM_HEADS + h + 1] + b_f)
        li_row = zgt[h:h + 1, :] + b_i
        lf_row = _log_sigmoid(zgt[N_MLSTM_HEADS + h:N_MLSTM_HEADS + h + 1, :] + b_f)
        if n_valid < t:
            li_col = jnp.where(valid_col, li_col, -jnp.inf)
            lf_col = jnp.where(valid_col, lf_col, 0.0)
            li_row = jnp.where(valid_row, li_row, -jnp.inf)
            lf_row = jnp.where(valid_row, lf_row, 0.0)
        hm, c_new, n_new, m_new = _mlstm_chunk(
            q, k, v, li_col, lf_col, li_row, lf_row, c_s[h], n_s[h:h + 1, :], m_s[h:h + 1, 0:1])
        c_s[h] = c_new
        n_s[h:h + 1, :] = n_new
        m_s[h:h + 1, :] = jnp.broadcast_to(m_new, (1, LANES))
        gate = jax.nn.sigmoid(og[:, h * MLSTM_DV:(h + 1) * MLSTM_DV])
        out = _rms(hm, gm_ref[h:h + 1, :]) * gate
        out = out[:rows].astype(o_ref.dtype)
        if len(o_ref.shape) == 3:
            o_ref[0, :, h * MLSTM_DV:(h + 1) * MLSTM_DV] = out
        else:
            o_ref[:, h * MLSTM_DV:(h + 1) * MLSTM_DV] = out

    @pl.when(step == last)
    def _():
        co_ref[0] = c_s[...]
        no_ref[0] = n_s[...]
        mo_ref[0] = m_s[...]


_MLSTM_SCRATCH = [pltpu.VMEM((N_MLSTM_HEADS, MLSTM_DV, MLSTM_DK), jnp.float32),
                  pltpu.VMEM((N_MLSTM_HEADS, MLSTM_DK), jnp.float32),
                  pltpu.VMEM((N_MLSTM_HEADS, LANES), jnp.float32)]


def _mlstm_state_shapes(n):
    return [jax.ShapeDtypeStruct((n, N_MLSTM_HEADS, MLSTM_DV, MLSTM_DK), jnp.float32),
            jax.ShapeDtypeStruct((n, N_MLSTM_HEADS, MLSTM_DK), jnp.float32),
            jax.ShapeDtypeStruct((n, N_MLSTM_HEADS, LANES), jnp.float32)]


def _mlstm_state_specs(index):
    return [pl.BlockSpec((1, N_MLSTM_HEADS, MLSTM_DV, MLSTM_DK), lambda *a: (index(*a), 0, 0, 0)),
            pl.BlockSpec((1, N_MLSTM_HEADS, MLSTM_DK), lambda *a: (index(*a), 0, 0)),
            pl.BlockSpec((1, N_MLSTM_HEADS, LANES), lambda *a: (index(*a), 0, 0))]


def _mlstm_prompt(z1, ztail, zg, zgt, bias, g_mlstm, batch, seq):
    t = MLSTM_T
    nc = seq // t
    row = lambda w: pl.BlockSpec((t, w), lambda n, c: (n * nc + c, 0))
    return pl.pallas_call(
        functools.partial(_mlstm_kernel, n_valid=t, has_init=False),
        grid=(batch, nc),
        in_specs=[pl.BlockSpec(memory_space=pltpu.SMEM),
                  row(HEAD_COLS), row(MLSTM_WIDTH), row(LANES),
                  pl.BlockSpec((8, t), lambda n, c: (0, n * nc + c)),
                  pl.BlockSpec((N_MLSTM_HEADS, MLSTM_DV), lambda n, c: (0, 0))],
        out_specs=[row(MLSTM_WIDTH)] + _mlstm_state_specs(lambda n, c: n),
        out_shape=[jax.ShapeDtypeStruct((z1.shape[0], MLSTM_WIDTH), jnp.bfloat16)] + _mlstm_state_shapes(batch),
        scratch_shapes=_MLSTM_SCRATCH,
        compiler_params=_cparams(("parallel", "arbitrary")),
        name="mlstm_prompt",
    )(bias, z1, ztail, zg, zgt, g_mlstm)


def _mlstm_sample(zs, zts, zgs, zgts, bias, g_mlstm, c0_all, n0, m0, n_valid, layer, c_out_prev):
    s, rows, _ = zs.shape
    depth = c0_all.shape[0]
    seq_blk = lambda r, c: pl.BlockSpec((1, r, c), lambda n, _: (n, 0, 0))
    c_blk = pl.BlockSpec((None, 1, N_MLSTM_HEADS, MLSTM_DV, MLSTM_DK), lambda n, _: (layer, n, 0, 0, 0))
    small_in = _mlstm_state_specs(lambda n, _: n)[1:]
    in_specs = [pl.BlockSpec(memory_space=pltpu.SMEM),
                seq_blk(rows, HEAD_COLS), seq_blk(rows, MLSTM_WIDTH), seq_blk(rows, LANES),
                seq_blk(8, MLSTM_T),
                pl.BlockSpec((N_MLSTM_HEADS, MLSTM_DV), lambda n, _: (0, 0)), c_blk] + small_in
    args = [bias, zs, zts, zgs, zgts, g_mlstm, c0_all, n0, m0]
    aliases = {}
    if c_out_prev is not None:
        in_specs.append(pl.BlockSpec(memory_space=pl.ANY))
        args.append(c_out_prev)
        aliases = {len(args) - 1: 1}
    return pl.pallas_call(
        functools.partial(_mlstm_kernel, n_valid=n_valid, has_init=True, has_prev=c_out_prev is not None),
        grid=(s, 1),
        in_specs=in_specs,
        out_specs=[seq_blk(rows, MLSTM_WIDTH), c_blk] + small_in,
        out_shape=[jax.ShapeDtypeStruct((s, rows, MLSTM_WIDTH), jnp.float32),
                   jax.ShapeDtypeStruct((depth,) + c0_all.shape[1:], jnp.float32)] + _mlstm_state_shapes(s)[1:],
        input_output_aliases=aliases,
        scratch_shapes=_MLSTM_SCRATCH,
        compiler_params=_cparams(("parallel", "arbitrary")),
        name="mlstm_sample",
    )(*args)


def _conv_norm_act(cv, cb, g, b):
    cv = cv + cb
    mu = jnp.mean(cv, axis=-1, keepdims=True)
    d = cv - mu
    var = jnp.mean(d * d, axis=-1, keepdims=True)
    y = d * lax.rsqrt(var + NORM_EPS) * g + b
    return y * jax.nn.sigmoid(y)


def _conv_prompt_kernel(ga_ref, gb_ref, w_ref, cb_ref, g_ref, b_ref, o_ref, st_ref, ext_ref, cv_ref, *, nt):
    tb, rsub = CONV_TB, CONV_RSUB
    i = pl.program_id(1)

    @pl.when(i == 0)
    def _():
        ext_ref[0:CONV_HIST, :] = jnp.zeros((CONV_HIST, CONV_CH), jnp.float32)

    ext_ref[CONV_HIST:, :] = ga_ref[...] * jax.nn.sigmoid(gb_ref[...])
    off = CONV_HIST - (CONV_WIDTH - 1)
    span = rsub + CONV_HIST
    for lc in range(CONV_CH // LANES):
        cols = slice(lc * LANES, (lc + 1) * LANES)
        for r0 in range(0, tb, rsub):
            e = ext_ref[r0:r0 + span, cols]
            acc = None
            for s in range(SUBLANES):
                zs = e if s == 0 else pltpu.roll(e, span - s, 0)
                for a in range(CONV_HIST // SUBLANES + 1):
                    j = SUBLANES * a + s - off
                    if 0 <= j < CONV_WIDTH:
                        term = w_ref[j:j + 1, cols] * zs[SUBLANES * a:SUBLANES * a + rsub]
                        acc = term if acc is None else acc + term
            cv_ref[r0:r0 + rsub, cols] = acc
    o_ref[...] = _conv_norm_act(cv_ref[...], cb_ref[...], g_ref[...], b_ref[...]).astype(o_ref.dtype)

    @pl.when(i == nt - 1)
    def _():
        st_ref[0] = ext_ref[tb + off:tb + CONV_HIST, :]

    ext_ref[0:CONV_HIST, :] = ext_ref[tb:tb + CONV_HIST, :]


def _conv_prompt(ztail, conv_w, conv_b, g_conv, b_conv, batch, seq):
    tb = CONV_TB
    nt = seq // tb
    vec = pl.BlockSpec((1, CONV_CH), lambda n, i: (0, 0))
    row = lambda col: pl.BlockSpec((tb, CONV_CH), lambda n, i: (n * nt + i, col))
    return pl.pallas_call(
        functools.partial(_conv_prompt_kernel, nt=nt),
        grid=(batch, nt),
        in_specs=[row(MLSTM_WIDTH // CONV_CH), row(MLSTM_WIDTH // CONV_CH + 1),
                  pl.BlockSpec((CONV_WIDTH, CONV_CH), lambda n, i: (0, 0)), vec, vec, vec],
        out_specs=[row(0), pl.BlockSpec((1, CONV_WIDTH - 1, CONV_CH), lambda n, i: (n, 0, 0))],
        out_shape=[jax.ShapeDtypeStruct((ztail.shape[0], CONV_CH), jnp.bfloat16),
                   jax.ShapeDtypeStruct((batch, CONV_WIDTH - 1, CONV_CH), jnp.float32)],
        scratch_shapes=[pltpu.VMEM((tb + CONV_HIST, CONV_CH), jnp.float32),
                        pltpu.VMEM((tb, CONV_CH), jnp.float32)],
        compiler_params=_cparams(("parallel", "arbitrary")),
        name="conv_prompt",
    )(ztail, ztail, conv_w, conv_b.reshape(1, -1), g_conv.reshape(1, -1), b_conv.reshape(1, -1))


def _conv_sample_kernel(ga_ref, gb_ref, st_ref, w_ref, cb_ref, g_ref, b_ref, o_ref, sto_ref, ext_ref, *, n_new):
    hist = CONV_WIDTH - 1
    ext_ref[0:hist] = st_ref[...]
    ext_ref[hist:] = ga_ref[...] * jax.nn.sigmoid(gb_ref[...])
    for t in range(n_new):
        acc = w_ref[0:1, :] * ext_ref[t]
        for j in range(1, CONV_WIDTH):
            acc = acc + w_ref[j:j + 1, :] * ext_ref[t + j]
        o_ref[t] = _conv_norm_act(acc, cb_ref[...], g_ref[...], b_ref[...])
    sto_ref[...] = ext_ref[n_new:]


def _conv_sample(ga, gb, state, conv_w, conv_b, g_conv, b_conv):
    n_new, s, ch = ga.shape
    hist = CONV_WIDTH - 1
    full = lambda shape: pl.BlockSpec(shape, lambda i: (0,) * len(shape))
    return pl.pallas_call(
        functools.partial(_conv_sample_kernel, n_new=n_new),
        grid=(1,),
        in_specs=[full((n_new, s, ch)), full((n_new, s, ch)), full((hist, s, ch)),
                  full((CONV_WIDTH, ch)), full((1, ch)), full((1, ch)), full((1, ch))],
        out_specs=[full((n_new, s, ch)), full((hist, s, ch))],
        out_shape=[jax.ShapeDtypeStruct((n_new, s, ch), jnp.float32),
                   jax.ShapeDtypeStruct((hist, s, ch), jnp.float32)],
        scratch_shapes=[pltpu.VMEM((hist + n_new, s, ch), jnp.float32)],
        compiler_params=_cparams(("arbitrary",)),
        name="conv_sample",
    )(ga, gb, state, conv_w, conv_b.reshape(1, -1), g_conv.reshape(1, -1), b_conv.reshape(1, -1))


def _layer(l, xp, xs, h, batch, seq, s_batch, s_len, ck, cv, c0_all, c_out_prev, n0, m0, conv_state,
           weights, vectors, g_next):
    w_in, w_out, w_up, w_down = weights
    (sinks, b_ig, b_fg, g_mlstm, conv_w, conv_b, g_conv, b_conv, g_post_mix, g_pre_mlp, g_post_mlp) = vectors
    mp = batch * seq
    rows = SUBLANES
    assert s_len <= rows
    f32, bf16 = jnp.float32, jnp.bfloat16

    z1 = _matmul_fullk([h], w_in, l, n_cols=HEAD_COLS, tn=3 * MXU_COLS, out_dtype=f32, tm_target=1040,
                       name="proj_in_head")
    zg = _matmul_fullk([h], w_in, l, n_cols=LANES, col_block0=COL_GATE // LANES, tn=LANES, out_dtype=f32,
                       name="proj_in_gate")
    w_tail = w_in[l:l + 1, :, COL_OG:]
    zt = _matmul_fullk([h], w_tail, 0, n_cols=TAIL_COLS, tn=2 * MXU_COLS, out_dtype=f32, name="proj_in_tail")
    zgt = zg[:, :8].T
    bias = jnp.stack([b_ig, b_fg])

    def pad_rows(a):
        a = a.reshape(s_batch, s_len, a.shape[-1])
        return jnp.pad(a, ((0, 0), (0, rows - s_len), (0, 0)))

    zs, zts, zgs = pad_rows(z1[mp:]), pad_rows(zt[mp:]), pad_rows(zg[mp:])
    zgts = jnp.pad(zg[mp:, :8].reshape(s_batch, s_len, 8).transpose(0, 2, 1),
                   ((0, 0), (0, 0), (0, MLSTM_T - s_len)))

    a_mix, kwin_p, vwin_p = _attn_prompt(z1, sinks, batch, seq)
    a_s, kwin_s, vwin_s = _attn_sample(zs, ck.reshape(s_batch, WINDOW, KV_WIDTH),
                                       cv.reshape(s_batch, WINDOW, KV_WIDTH), sinks, s_len)
    b_mix, c_p, n_p, m_p = _mlstm_prompt(z1, zt, zg, zgt, bias, g_mlstm, batch, seq)
    m0b = jnp.broadcast_to(m0[:, :, None], m0.shape + (LANES,))
    b_s, c_s_all, n_s, m_s = _mlstm_sample(zs, zts, zgs, zgts, bias, g_mlstm, c0_all, n0, m0b, s_len, l, c_out_prev)
    c_mix, st_p = _conv_prompt(zt, conv_w, conv_b, g_conv, b_conv, batch, seq)
    tm = lambda a: a.reshape(s_batch, s_len, CONV_CH).transpose(1, 0, 2)
    ga_s = tm(zt[mp:, MLSTM_WIDTH:MLSTM_WIDTH + CONV_CH])
    gb_s = tm(zt[mp:, MLSTM_WIDTH + CONV_CH:])
    c_sm, st_s = _conv_sample(ga_s, gb_s, conv_state.transpose(1, 0, 2), conv_w, conv_b, g_conv, b_conv)

    unpad = lambda a: a[:, :s_len].reshape(s_batch * s_len, a.shape[-1]).astype(bf16)
    a_mix = lax.dynamic_update_slice(a_mix, unpad(a_s), (mp, 0))
    b_mix = lax.dynamic_update_slice(b_mix, unpad(b_s), (mp, 0))
    c_mix = lax.dynamic_update_slice(
        c_mix, c_sm.transpose(1, 0, 2).reshape(s_batch * s_len, CONV_CH).astype(bf16), (mp, 0))

    y = _matmul_fullk([a_mix, b_mix, c_mix], w_out, l, n_cols=D_MODEL, tn=2 * MXU_COLS, out_dtype=f32,
                      name="proj_out")
    xp, xs, h2 = _resid(y, xp, xs, g_post_mix, g_pre_mlp)
    ff = _matmul_fullk([h2], w_up, l, n_cols=w_up.shape[2], tn=2 * MXU_COLS, out_dtype=bf16, relu2=True,
                       name="mlp_up")
    y2 = _matmul([ff], w_down, l, n_cols=D_MODEL, tn=4 * MXU_COLS, out_dtype=f32, name="mlp_down")
    xp, xs, h_next = _resid(y2, xp, xs, g_post_mlp, g_next)

    kv = lambda a, n: a.reshape(n, WINDOW, N_KV_HEADS, HEAD_DIM)
    prompt_state = (kv(kwin_p, batch), kv(vwin_p, batch), c_p, n_p, m_p[:, :, 0], st_p)
    sample_state = (kv(kwin_s, s_batch), kv(vwin_s, s_batch), n_s, m_s[:, :, 0], st_s.transpose(1, 0, 2))
    return xp, xs, h_next, prompt_state, sample_state, c_s_all


def kernel(x_prompt, x_sample, cache_win_k, cache_win_v, state_mlstm_C, state_mlstm_n, state_mlstm_m, state_conv,
           g_pre_mix, w_in, attn_sinks, b_igate, b_fgate, g_mlstm, conv_w, conv_b, g_conv, b_conv,
           w_out, g_post_mix, g_pre_mlp, w_up, w_down, g_post_mlp):
    batch, seq, d = x_prompt.shape
    s_batch, s_len, _ = x_sample.shape
    depth = w_in.shape[0]
    xp = x_prompt.reshape(batch * seq, d)
    xs = x_sample.reshape(s_batch * s_len, d)
    h = _norm(xp, xs, g_pre_mix[0])
    p_states, s_states, s_c = [], [], None
    for l in range(depth):
        vectors = (attn_sinks[l], b_igate[l], b_fgate[l], g_mlstm[l], conv_w[l], conv_b[l],
                   g_conv[l], b_conv[l], g_post_mix[l], g_pre_mlp[l], g_post_mlp[l])
        g_next = g_pre_mix[l + 1] if l + 1 < depth else None
        xp, xs, h, ps, ss, s_c = _layer(l, xp, xs, h, batch, seq, s_batch, s_len, cache_win_k[l], cache_win_v[l],
                                        state_mlstm_C, s_c, state_mlstm_n[l], state_mlstm_m[l], state_conv[l],
                                        (w_in, w_out, w_up, w_down), vectors, g_next)
        p_states.append(ps)
        s_states.append(ss)
    stack = lambda states, i: jnp.stack([st[i] for st in states])
    s_k, s_v, s_n, s_m, s_cv = (stack(s_states, i) for i in range(5))
    return ((xp.reshape(batch, seq, d), xs.reshape(s_batch, s_len, d))
            + tuple(stack(p_states, i) for i in range(6))
            + (s_k, s_v, s_c, s_n, s_m, s_cv))
```

```python
import functools
import math

import jax
import jax.numpy as jnp
from jax import lax
from jax.experimental import pallas as pl
from jax.experimental.pallas import tpu as pltpu

D_MODEL = 4096
HEAD_DIM = 64
N_Q_HEADS = 16
N_KV_HEADS = 2
Q_PER_KV = N_Q_HEADS // N_KV_HEADS
WINDOW = 128
ROPE_THETA = 10000.0
N_MLSTM_HEADS = 4
MLSTM_DV = 512
MLSTM_DK = 256
CONV_CH = 1024
CONV_WIDTH = 31
NORM_EPS = 1e-6
PAST_LEN = 16384

ATT_WIDTH = N_Q_HEADS * HEAD_DIM
KV_WIDTH = N_KV_HEADS * HEAD_DIM
MLSTM_WIDTH = N_MLSTM_HEADS * MLSTM_DV
COL_QA = 0
COL_KA = COL_QA + ATT_WIDTH
COL_VA = COL_KA + KV_WIDTH
COL_QM = COL_VA + KV_WIDTH
COL_KM = COL_QM + N_MLSTM_HEADS * MLSTM_DK
COL_VM = COL_KM + N_MLSTM_HEADS * MLSTM_DK
COL_GATE = COL_VM + MLSTM_WIDTH
COL_OG = COL_GATE + 2 * N_MLSTM_HEADS
TAIL_COLS = MLSTM_WIDTH + 2 * CONV_CH

LANES = 128
SUBLANES = 8
MXU_COLS = 256
Z1_COLS = 5632
MLSTM_T_PROMPT = 256
MLSTM_T_SAMPLE = 128
CONV_TB = 256
CONV_RSUB = 64
CONV_HIST = 32
VMEM_LIMIT = 56 * 1024 * 1024


def _largest_tile(total, target, multiple):
    best = None
    for t in range(multiple, min(total, target) + 1, multiple):
        if total % t == 0:
            best = t
    assert best is not None, (total, target, multiple)
    return best


def _cparams(sem):
    return pltpu.CompilerParams(dimension_semantics=sem, vmem_limit_bytes=VMEM_LIMIT)


def _rms(x, g):
    return x * lax.rsqrt(jnp.mean(x * x, axis=-1, keepdims=True) + NORM_EPS) * g


def _row_tile(mp, ms):
    return _largest_tile(math.gcd(mp, ms), 256, 16)


def _split_specs(tr, d, nbp):
    prompt = pl.BlockSpec((tr, d), lambda i: (jnp.minimum(i, nbp - 1), 0))
    sample = pl.BlockSpec((tr, d), lambda i: (jnp.maximum(i - nbp, 0), 0))
    return prompt, sample


def _norm_kernel(xp_ref, xs_ref, g_ref, h_ref, *, nbp):
    i = pl.program_id(0)

    @pl.when(i < nbp)
    def _():
        h_ref[...] = _rms(xp_ref[...], g_ref[...]).astype(h_ref.dtype)

    @pl.when(i >= nbp)
    def _():
        h_ref[...] = _rms(xs_ref[...], g_ref[...]).astype(h_ref.dtype)


def _norm(xp, xs, g):
    (mp, d), ms = xp.shape, xs.shape[0]
    tr = _row_tile(mp, ms)
    nbp = mp // tr
    prompt, sample = _split_specs(tr, d, nbp)
    return pl.pallas_call(
        functools.partial(_norm_kernel, nbp=nbp),
        grid=((mp + ms) // tr,),
        in_specs=[prompt, sample, pl.BlockSpec((1, d), lambda i: (0, 0))],
        out_specs=pl.BlockSpec((tr, d), lambda i: (i, 0)),
        out_shape=jax.ShapeDtypeStruct((mp + ms, d), jnp.bfloat16),
        compiler_params=_cparams(("arbitrary",)),
        name="rmsnorm",
    )(xp, xs, g.reshape(1, d))


def _resid_kernel(y_ref, xp_ref, xs_ref, gpost_ref, *rest, nbp, with_next):
    if with_next:
        gnext_ref, xop_ref, xos_ref, h_ref = rest
    else:
        xop_ref, xos_ref = rest
    i = pl.program_id(0)

    def update(x_ref, xo_ref):
        xn = x_ref[...] + _rms(y_ref[...], gpost_ref[...])
        xo_ref[...] = xn
        if with_next:
            h_ref[...] = _rms(xn, gnext_ref[...]).astype(h_ref.dtype)

    pl.when(i < nbp)(lambda: update(xp_ref, xop_ref))
    pl.when(i >= nbp)(lambda: update(xs_ref, xos_ref))


def _resid(y, xp, xs, g_post, g_next):
    (mp, d), ms = xp.shape, xs.shape[0]
    tr = _row_tile(mp, ms)
    nbp = mp // tr
    prompt, sample = _split_specs(tr, d, nbp)
    row = pl.BlockSpec((tr, d), lambda i: (i, 0))
    vec = pl.BlockSpec((1, d), lambda i: (0, 0))
    with_next = g_next is not None
    out_specs = [prompt, sample] + ([row] if with_next else [])
    out_shape = [jax.ShapeDtypeStruct((mp, d), jnp.float32), jax.ShapeDtypeStruct((ms, d), jnp.float32)]
    args = [y, xp, xs, g_post.reshape(1, d)]
    if with_next:
        out_shape.append(jax.ShapeDtypeStruct((mp + ms, d), jnp.bfloat16))
        args.append(g_next.reshape(1, d))
    out = pl.pallas_call(
        functools.partial(_resid_kernel, nbp=nbp, with_next=with_next),
        grid=((mp + ms) // tr,),
        in_specs=[row, prompt, sample, vec] + ([vec] if with_next else []),
        out_specs=out_specs, out_shape=out_shape,
        compiler_params=_cparams(("arbitrary",)),
        name="resid_norm" if with_next else "resid_last",
    )(*args)
    return (out[0], out[1], out[2]) if with_next else (out[0], out[1], None)


def _mm_kernel(*refs, n_a, k_bounds, nk, relu2):
    a_refs, b_ref, o_ref, scratch = refs[:n_a], refs[n_a], refs[n_a + 1], refs[n_a + 2:]
    acc_ref = scratch[0] if scratch else o_ref
    k = pl.program_id(2)

    for t, a_ref in enumerate(a_refs):
        lo, hi = k_bounds[t], k_bounds[t + 1]

        def part(a_ref=a_ref):
            return jnp.dot(a_ref[...], b_ref[...].astype(jnp.bfloat16), preferred_element_type=jnp.float32)

        def assign(part=part):
            acc_ref[...] = part()

        def accumulate(part=part):
            acc_ref[...] += part()

        if lo == 0:
            pl.when(k == 0)(assign)
            if hi > 1:
                pl.when((k > 0) & (k < hi))(accumulate)
        else:
            pl.when((k >= lo) & (k < hi))(accumulate)

    if scratch or relu2:
        @pl.when(k == nk - 1)
        def _():
            r = acc_ref[...]
            if relu2:
                r = jnp.square(jnp.maximum(r, 0.0))
            o_ref[...] = r.astype(o_ref.dtype)


def _matmul(a_list, b, layer, *, n_cols, tn, out_dtype, relu2=False, name):
    m = a_list[0].shape[0]
    tm = _largest_tile(m, 2080, 16)
    tk = _largest_tile(math.gcd(*[a.shape[1] for a in a_list]), 1024, LANES)
    k_bounds = [0]
    for a in a_list:
        k_bounds.append(k_bounds[-1] + a.shape[1] // tk)
    nk = k_bounds[-1]
    assert nk * tk == b.shape[1] and n_cols % tn == 0

    def a_spec(lo, hi):
        return pl.BlockSpec((tm, tk), lambda i, j, k: (i, jnp.clip(k - lo, 0, hi - lo - 1)))

    scratch = [] if out_dtype == jnp.float32 else [pltpu.VMEM((tm, tn), jnp.float32)]
    return pl.pallas_call(
        functools.partial(_mm_kernel, n_a=len(a_list), k_bounds=tuple(k_bounds), nk=nk, relu2=relu2),
        grid=(m // tm, n_cols // tn, nk),
        in_specs=[a_spec(k_bounds[t], k_bounds[t + 1]) for t in range(len(a_list))]
                 + [pl.BlockSpec((None, tk, tn), lambda i, j, k: (layer, k, j))],
        out_specs=pl.BlockSpec((tm, tn), lambda i, j, k: (i, j)),
        out_shape=jax.ShapeDtypeStruct((m, n_cols), out_dtype),
        scratch_shapes=scratch,
        compiler_params=_cparams(("parallel", "parallel", "arbitrary")),
        name=name,
    )(*a_list, b)


def _mm_fullk_kernel(*refs, n_a, relu2, b_transposed):
    a_refs, b_ref, o_ref = refs[:n_a], refs[n_a], refs[n_a + 1]
    acc, lo = None, 0
    for a_ref in a_refs:
        width = a_ref.shape[1]
        if b_transposed:
            part = lax.dot_general(a_ref[...], b_ref[:, lo:lo + width].astype(jnp.bfloat16),
                                   (((1,), (1,)), ((), ())), preferred_element_type=jnp.float32)
        else:
            part = jnp.dot(a_ref[...], b_ref[lo:lo + width, :].astype(jnp.bfloat16),
                           preferred_element_type=jnp.float32)
        acc = part if acc is None else acc + part
        lo += width
    if relu2:
        acc = jnp.square(jnp.maximum(acc, 0.0))
    o_ref[...] = acc.astype(o_ref.dtype)


def _matmul_fullk(a_list, b, layer, *, n_cols, col0=0, tn, out_dtype, relu2=False, b_transposed=False, name):
    m = a_list[0].shape[0]
    kdim = sum(a.shape[1] for a in a_list)
    tm = _largest_tile(m, 2080, 16)
    assert n_cols % tn == 0
    if b_transposed:
        assert kdim == b.shape[2] and col0 % SUBLANES == 0
        b_spec = pl.BlockSpec((None, pl.Element(tn), pl.Element(kdim)),
                              lambda i, j: (layer, pl.multiple_of(col0 + j * tn, SUBLANES), 0))
    else:
        assert kdim == b.shape[1] and col0 % tn == 0
        b_spec = pl.BlockSpec((None, kdim, tn), lambda i, j: (layer, 0, j + col0 // tn))
    return pl.pallas_call(
        functools.partial(_mm_fullk_kernel, n_a=len(a_list), relu2=relu2, b_transposed=b_transposed),
        grid=(m // tm, n_cols // tn),
        in_specs=[pl.BlockSpec((tm, a.shape[1]), lambda i, j: (i, 0), pipeline_mode=pl.Buffered(1))
                  for a in a_list] + [b_spec],
        out_specs=pl.BlockSpec((tm, tn), lambda i, j: (i, j)),
        out_shape=jax.ShapeDtypeStruct((m, n_cols), out_dtype),
        compiler_params=_cparams(("parallel", "arbitrary")),
        name=name,
    )(*a_list, b)


def _rope_tables(pos0, length):
    half = HEAD_DIM // 2
    inv_freq = ROPE_THETA ** (-jnp.arange(half, dtype=jnp.float32) / half)
    pos = (pos0 + jnp.arange(length, dtype=jnp.int32)).astype(jnp.float32)
    ang = pos[:, None] * inv_freq[None, :]
    cos, sin = jnp.cos(ang), jnp.sin(ang)
    reps = LANES // HEAD_DIM
    cos_t = jnp.tile(jnp.concatenate([cos, cos], axis=-1), (1, reps))
    sin_t = jnp.tile(jnp.concatenate([-sin, sin], axis=-1), (1, reps))
    return cos_t, sin_t


def _rope(x, cos_t, sin_t):
    half = HEAD_DIM // 2
    lane = lax.broadcasted_iota(jnp.int32, x.shape, 1)
    swapped = jnp.where((lane % HEAD_DIM) < half,
                        pltpu.roll(x, LANES - half, 1), pltpu.roll(x, half, 1))
    return x * cos_t + swapped * sin_t


def _rope_wide(x, cos_t, sin_t):
    return jnp.concatenate(
        [_rope(x[:, c * LANES:(c + 1) * LANES], cos_t, sin_t) for c in range(x.shape[1] // LANES)], axis=-1)


def _attend(q_rot, k_ext, v_ext, mask_fn, sink_ref):
    rq, s_len = q_rot.shape[0], k_ext.shape[0]
    kb = k_ext.astype(jnp.bfloat16)
    vb = v_ext.astype(jnp.bfloat16)
    scale = HEAD_DIM ** -0.5
    rows = lax.broadcasted_iota(jnp.int32, (Q_PER_KV * rq, s_len), 0)
    mask = mask_fn(rows % rq, lax.broadcasted_iota(jnp.int32, (Q_PER_KV * rq, s_len), 1))
    outs = []
    for g in range(N_KV_HEADS):
        kg = kb[:, g * HEAD_DIM:(g + 1) * HEAD_DIM]
        vg = vb[:, g * HEAD_DIM:(g + 1) * HEAD_DIM]
        heads = range(g * Q_PER_KV, (g + 1) * Q_PER_KV)
        qg = jnp.concatenate([q_rot[:, h * HEAD_DIM:(h + 1) * HEAD_DIM] for h in heads], axis=0)
        sink = jnp.concatenate([jnp.full((rq, 1), sink_ref[0, h], jnp.float32) for h in heads], axis=0)
        s = lax.dot_general((qg * scale).astype(jnp.bfloat16), kg, (((1,), (1,)), ((), ())),
                            preferred_element_type=jnp.float32)
        s = jnp.where(mask, s, -jnp.inf)
        m = jnp.maximum(jnp.max(s, axis=-1, keepdims=True), sink)
        p = jnp.exp(s - m)
        denom = jnp.sum(p, axis=-1, keepdims=True) + jnp.exp(sink - m)
        o = jnp.dot(p.astype(jnp.bfloat16), vg, preferred_element_type=jnp.float32) / denom
        outs.extend(o[hh * rq:(hh + 1) * rq] for hh in range(Q_PER_KV))
    return jnp.concatenate(outs, axis=-1)


def _fill_rows(sample_rows, block_rows):
    ms = sample_rows.shape[0]
    padded = -(-ms // block_rows) * block_rows
    return jnp.pad(sample_rows.astype(jnp.bfloat16), ((0, padded - ms), (0, 0)))


def _attn_prompt_kernel(sink_ref, q_ref, kc_ref, kp_ref, vc_ref, vp_ref,
                        cosc_ref, sinc_ref, cosp_ref, sinp_ref, fill_ref,
                        o_ref, kwin_ref, vwin_ref, *, nb, nbp):
    i = pl.program_id(0)
    j = i % nb

    @pl.when(i < nbp)
    def _():
        cosc, sinc = cosc_ref[...], sinc_ref[...]
        q_rot = _rope_wide(q_ref[...], cosc, sinc)
        k_cur = _rope(kc_ref[...], cosc, sinc)
        k_prev = _rope(kp_ref[...], cosp_ref[...], sinp_ref[...])
        k_ext = jnp.concatenate([k_prev, k_cur], axis=0)
        v_ext = jnp.concatenate([vp_ref[...], vc_ref[...]], axis=0)

        def mask_fn(tq, sk):
            return ((sk >= tq) & (sk < WINDOW) & (j > 0)) | ((sk >= WINDOW) & ((sk - WINDOW) <= tq))

        o_ref[...] = _attend(q_rot, k_ext, v_ext, mask_fn, sink_ref).astype(o_ref.dtype)

        @pl.when(j == nb - 1)
        def _():
            kwin_ref[0] = k_cur
            vwin_ref[0] = vc_ref[...]

    @pl.when(i >= nbp)
    def _():
        o_ref[...] = fill_ref[...]


def _attn_prompt(z1, sinks, a_sample, batch, seq):
    nb = seq // WINDOW
    nbp = batch * nb
    fill = _fill_rows(a_sample, WINDOW)
    nbs = fill.shape[0] // WINDOW
    cos_t, sin_t = _rope_tables(0, seq)
    cur = lambda i: jnp.minimum(i, nbp - 1)
    prev = lambda i: cur(i) - (cur(i) % nb > 0)
    blk = lambda col: pl.BlockSpec((WINDOW, LANES), lambda i: (cur(i), col))
    blk_prev = lambda col: pl.BlockSpec((WINDOW, LANES), lambda i: (prev(i), col))
    tab = pl.BlockSpec((WINDOW, LANES), lambda i: (cur(i) % nb, 0))
    tab_prev = pl.BlockSpec((WINDOW, LANES), lambda i: (prev(i) % nb, 0))
    win = pl.BlockSpec((1, WINDOW, KV_WIDTH), lambda i: (cur(i) // nb, 0, 0))
    return pl.pallas_call(
        functools.partial(_attn_prompt_kernel, nb=nb, nbp=nbp),
        grid=(nbp + nbs,),
        in_specs=[pl.BlockSpec(memory_space=pltpu.SMEM),
                  pl.BlockSpec((WINDOW, ATT_WIDTH), lambda i: (cur(i), 0)),
                  blk(COL_KA // LANES), blk_prev(COL_KA // LANES),
                  blk(COL_VA // LANES), blk_prev(COL_VA // LANES),
                  tab, tab, tab_prev, tab_prev,
                  pl.BlockSpec((WINDOW, ATT_WIDTH), lambda i: (jnp.maximum(i - nbp, 0), 0))],
        out_specs=[pl.BlockSpec((WINDOW, ATT_WIDTH), lambda i: (i, 0)), win, win],
        out_shape=[jax.ShapeDtypeStruct((z1.shape[0], ATT_WIDTH), jnp.bfloat16),
                   jax.ShapeDtypeStruct((batch, WINDOW, KV_WIDTH), jnp.float32),
                   jax.ShapeDtypeStruct((batch, WINDOW, KV_WIDTH), jnp.float32)],
        compiler_params=_cparams(("arbitrary",)),
        name="attn_prompt",
    )(sinks.reshape(1, N_Q_HEADS), z1, z1, z1, z1, z1, cos_t, sin_t, cos_t, sin_t, fill)


def _attn_sample_kernel(sink_ref, z_ref, ck_ref, cv_ref, cos_ref, sin_ref,
                        o_ref, kout_ref, vout_ref, kext_ref, vext_ref, *, n_new, rows):
    cos_t, sin_t = cos_ref[...], sin_ref[...]
    z = z_ref[0]
    q_rot = _rope_wide(z[:, COL_QA:COL_QA + ATT_WIDTH], cos_t, sin_t)
    k_new = _rope(z[:, COL_KA:COL_KA + KV_WIDTH], cos_t, sin_t)
    v_new = z[:, COL_VA:COL_VA + KV_WIDTH]
    kext_ref[0:WINDOW, :] = ck_ref[0]
    vext_ref[0:WINDOW, :] = cv_ref[0]
    kext_ref[WINDOW:WINDOW + rows, :] = k_new
    vext_ref[WINDOW:WINDOW + rows, :] = v_new
    fill = jnp.zeros((WINDOW - rows, KV_WIDTH), jnp.float32)
    kext_ref[WINDOW + rows:, :] = fill
    vext_ref[WINDOW + rows:, :] = fill

    def mask_fn(tq, sk):
        return ((sk >= tq) & (sk < WINDOW)) | ((sk >= WINDOW) & ((sk - WINDOW) <= tq) & (sk < WINDOW + n_new))

    o_ref[0] = _attend(q_rot, kext_ref[...], vext_ref[...], mask_fn, sink_ref)
    kout_ref[0] = kext_ref[pl.ds(n_new, WINDOW), :]
    vout_ref[0] = vext_ref[pl.ds(n_new, WINDOW), :]


def _attn_sample(zs, cache_k, cache_v, sinks, n_new):
    s, rows, _ = zs.shape
    cos_t, sin_t = _rope_tables(PAST_LEN, rows)
    seq_blk = lambda r, c: pl.BlockSpec((1, r, c), lambda n: (n, 0, 0))
    tab = pl.BlockSpec((rows, LANES), lambda n: (0, 0))
    return pl.pallas_call(
        functools.partial(_attn_sample_kernel, n_new=n_new, rows=rows),
        grid=(s,),
        in_specs=[pl.BlockSpec(memory_space=pltpu.SMEM),
                  seq_blk(rows, Z1_COLS), seq_blk(WINDOW, KV_WIDTH), seq_blk(WINDOW, KV_WIDTH), tab, tab],
        out_specs=[seq_blk(rows, ATT_WIDTH), seq_blk(WINDOW, KV_WIDTH), seq_blk(WINDOW, KV_WIDTH)],
        out_shape=[jax.ShapeDtypeStruct((s, rows, ATT_WIDTH), jnp.float32),
                   jax.ShapeDtypeStruct((s, WINDOW, KV_WIDTH), jnp.float32),
                   jax.ShapeDtypeStruct((s, WINDOW, KV_WIDTH), jnp.float32)],
        scratch_shapes=[pltpu.VMEM((2 * WINDOW, KV_WIDTH), jnp.float32),
                        pltpu.VMEM((2 * WINDOW, KV_WIDTH), jnp.float32)],
        compiler_params=_cparams(("parallel",)),
        name="attn_sample",
    )(sinks.reshape(1, N_Q_HEADS), zs, cache_k, cache_v, cos_t, sin_t)


def _log_sigmoid(x):
    return jnp.minimum(x, 0.0) - jnp.log(1.0 + jnp.exp(-jnp.abs(x)))


def _mlstm_chunk(q, k, v, li_col, lf_col, li_row, lf_row, c_state, n_state, m_state):
    t = q.shape[0]
    r = lax.broadcasted_iota(jnp.int32, (t, t), 0)
    c = lax.broadcasted_iota(jnp.int32, (t, t), 1)
    tril = c <= r
    b_col = jnp.sum(jnp.where(tril, lf_row, 0.0), axis=1, keepdims=True)
    b_row = jnp.sum(jnp.where(r <= c, lf_col, 0.0), axis=0, keepdims=True)
    b_last = jnp.sum(lf_row, axis=1, keepdims=True)
    log_d = jnp.where(tril, b_col - b_row + li_row, -jnp.inf)
    inter = b_col + m_state
    m_row = jnp.maximum(inter, jnp.max(log_d, axis=1, keepdims=True))
    qb = q.astype(jnp.bfloat16)
    kb = k.astype(jnp.bfloat16)
    vb = v.astype(jnp.bfloat16)
    qk = lax.dot_general(qb, kb, (((1,), (1,)), ((), ())), preferred_element_type=jnp.float32)
    s = qk * jnp.exp(log_d - m_row)
    w_inter = jnp.exp(inter - m_row)
    cq = lax.dot_general(qb, c_state.astype(jnp.bfloat16), (((1,), (1,)), ((), ())),
                         preferred_element_type=jnp.float32)
    num = jnp.dot(s.astype(jnp.bfloat16), vb, preferred_element_type=jnp.float32) + w_inter * cq
    den = jnp.sum(s, axis=1, keepdims=True) + w_inter * jnp.sum(q * n_state, axis=1, keepdims=True)
    h = num / jnp.maximum(jnp.abs(den), jnp.exp(-m_row))
    g_row = b_last - b_row + li_row
    g_col = b_last - b_col + li_col
    m_new = jnp.maximum(b_last + m_state, jnp.max(g_row, axis=1, keepdims=True))
    decay = jnp.exp(b_last + m_state - m_new)
    wk_col = jnp.exp(g_col - m_new)
    wv_t = (wk_col * v).T.astype(jnp.bfloat16)
    c_new = decay * c_state + jnp.dot(wv_t, kb, preferred_element_type=jnp.float32)
    n_new = decay * n_state + jnp.sum(wk_col * k, axis=0, keepdims=True)
    return h, c_new, n_new, m_new


def _mlstm_heads(bias_ref, z, og, zgt, gm_ref, c_s, n_s, m_s, write_out, *, t, n_valid):
    rows = z.shape[0]
    zg = z[:, COL_GATE:COL_GATE + LANES]
    if rows < t:
        pad = lambda a: jnp.concatenate([a, jnp.zeros((t - rows, a.shape[1]), a.dtype)], axis=0)
        z, og, zg = pad(z), pad(og), pad(zg)
    valid_col = lax.broadcasted_iota(jnp.int32, (t, 1), 0) < n_valid
    valid_row = lax.broadcasted_iota(jnp.int32, (1, t), 1) < n_valid
    for h in range(N_MLSTM_HEADS):
        q = z[:, COL_QM + h * MLSTM_DK:COL_QM + (h + 1) * MLSTM_DK]
        k = z[:, COL_KM + h * MLSTM_DK:COL_KM + (h + 1) * MLSTM_DK] * (MLSTM_DK ** -0.5)
        v = z[:, COL_VM + h * MLSTM_DV:COL_VM + (h + 1) * MLSTM_DV]
        b_i = bias_ref[0, h]
        b_f = bias_ref[1, h]
        li_col = zg[:, h:h + 1] + b_i
        lf_col = _log_sigmoid(zg[:, N_MLSTM_HEADS + h:N_MLSTM_HEADS + h + 1] + b_f)
        li_row = zgt[h:h + 1, :] + b_i
        lf_row = _log_sigmoid(zgt[N_MLSTM_HEADS + h:N_MLSTM_HEADS + h + 1, :] + b_f)
        if n_valid < t:
            li_col = jnp.where(valid_col, li_col, -jnp.inf)
            lf_col = jnp.where(valid_col, lf_col, 0.0)
            li_row = jnp.where(valid_row, li_row, -jnp.inf)
            lf_row = jnp.where(valid_row, lf_row, 0.0)
        hm, c_new, n_new, m_new = _mlstm_chunk(
            q, k, v, li_col, lf_col, li_row, lf_row, c_s[h], n_s[h:h + 1, :], m_s[h:h + 1, 0:1])
        c_s[h] = c_new
        n_s[h:h + 1, :] = n_new
        m_s[h:h + 1, :] = jnp.broadcast_to(m_new, (1, LANES))
        gate = jax.nn.sigmoid(og[:, h * MLSTM_DV:(h + 1) * MLSTM_DV])
        write_out(h, (_rms(hm, gm_ref[h:h + 1, :]) * gate)[:rows])


def _mlstm_prompt_kernel(bias_ref, z_ref, og_ref, zgt_ref, gm_ref, fill_ref,
                         o_ref, co_ref, no_ref, mo_ref, c_s, n_s, m_s, *, t, nc, nbp):
    i = pl.program_id(0)
    step = i % nc

    @pl.when(i < nbp)
    def _():
        @pl.when(step == 0)
        def _():
            c_s[...] = jnp.zeros_like(c_s)
            n_s[...] = jnp.zeros_like(n_s)
            m_s[...] = jnp.zeros_like(m_s)

        def write_out(h, out):
            o_ref[:, h * MLSTM_DV:(h + 1) * MLSTM_DV] = out.astype(o_ref.dtype)

        _mlstm_heads(bias_ref, z_ref[...], og_ref[...], zgt_ref[...], gm_ref, c_s, n_s, m_s, write_out,
                     t=t, n_valid=t)

        @pl.when(step == nc - 1)
        def _():
            co_ref[0] = c_s[...]
            no_ref[0] = n_s[...]
            mo_ref[0] = m_s[...]

    @pl.when(i >= nbp)
    def _():
        o_ref[...] = fill_ref[...]


def _mlstm_sample_kernel(bias_ref, z_ref, og_ref, zgt_ref, gm_ref, c0_ref, n0_ref, m0_ref, *rest,
                         t, n_valid, layer, first_call):
    o_ref, co_ref, no_ref, mo_ref, c_s, n_s, m_s = rest if first_call else rest[1:]
    c_s[...] = c0_ref[0]
    n_s[...] = n0_ref[0]
    m_s[...] = m0_ref[0]

    def write_out(h, out):
        o_ref[0, :, h * MLSTM_DV:(h + 1) * MLSTM_DV] = out

    _mlstm_heads(bias_ref, z_ref[0], og_ref[0], zgt_ref[0], gm_ref, c_s, n_s, m_s, write_out,
                 t=t, n_valid=n_valid)
    if first_call:
        for other in range(co_ref.shape[0]):
            co_ref[other, 0] = c_s[...] if other == layer else jnp.zeros_like(c_s)
    else:
        co_ref[0] = c_s[...]
    no_ref[0] = n_s[...]
    mo_ref[0] = m_s[...]


_MLSTM_SCRATCH = [pltpu.VMEM((N_MLSTM_HEADS, MLSTM_DV, MLSTM_DK), jnp.float32),
                  pltpu.VMEM((N_MLSTM_HEADS, MLSTM_DK), jnp.float32),
                  pltpu.VMEM((N_MLSTM_HEADS, LANES), jnp.float32)]


def _mlstm_state_shapes(n):
    return [jax.ShapeDtypeStruct((n, N_MLSTM_HEADS, MLSTM_DV, MLSTM_DK), jnp.float32),
            jax.ShapeDtypeStruct((n, N_MLSTM_HEADS, MLSTM_DK), jnp.float32),
            jax.ShapeDtypeStruct((n, N_MLSTM_HEADS, LANES), jnp.float32)]


def _mlstm_state_specs(index):
    return [pl.BlockSpec((1, N_MLSTM_HEADS, MLSTM_DV, MLSTM_DK), lambda i: (index(i), 0, 0, 0)),
            pl.BlockSpec((1, N_MLSTM_HEADS, MLSTM_DK), lambda i: (index(i), 0, 0)),
            pl.BlockSpec((1, N_MLSTM_HEADS, LANES), lambda i: (index(i), 0, 0))]


def _mlstm_prompt(z1, ztail, zgt, bias, g_mlstm, b_sample, batch, seq):
    t = MLSTM_T_PROMPT
    nc = seq // t
    nbp = batch * nc
    fill = _fill_rows(b_sample, t)
    nbs = fill.shape[0] // t
    cur = lambda i: jnp.minimum(i, nbp - 1)
    row = lambda w: pl.BlockSpec((t, w), lambda i: (cur(i), 0))
    return pl.pallas_call(
        functools.partial(_mlstm_prompt_kernel, t=t, nc=nc, nbp=nbp),
        grid=(nbp + nbs,),
        in_specs=[pl.BlockSpec(memory_space=pltpu.SMEM),
                  row(Z1_COLS), row(MLSTM_WIDTH),
                  pl.BlockSpec((SUBLANES, t), lambda i: (0, cur(i))),
                  pl.BlockSpec((N_MLSTM_HEADS, MLSTM_DV), lambda i: (0, 0)),
                  pl.BlockSpec((t, MLSTM_WIDTH), lambda i: (jnp.maximum(i - nbp, 0), 0))],
        out_specs=[pl.BlockSpec((t, MLSTM_WIDTH), lambda i: (i, 0))] + _mlstm_state_specs(lambda i: cur(i) // nc),
        out_shape=[jax.ShapeDtypeStruct((z1.shape[0], MLSTM_WIDTH), jnp.bfloat16)] + _mlstm_state_shapes(batch),
        scratch_shapes=_MLSTM_SCRATCH,
        compiler_params=_cparams(("arbitrary",)),
        name="mlstm_prompt",
    )(bias, z1, ztail, zgt, g_mlstm, fill)


def _mlstm_sample(zs, zts, zgts, bias, g_mlstm, c0_all, n0, m0, n_valid, layer, c_out_prev):
    s, rows, _ = zs.shape
    t = MLSTM_T_SAMPLE
    depth = c0_all.shape[0]
    first_call = c_out_prev is None
    seq_blk = lambda r, c: pl.BlockSpec((1, r, c), lambda n: (n, 0, 0))
    c_in = pl.BlockSpec((None, 1, N_MLSTM_HEADS, MLSTM_DV, MLSTM_DK), lambda n: (layer, n, 0, 0, 0))
    c_out = (pl.BlockSpec((depth, 1, N_MLSTM_HEADS, MLSTM_DV, MLSTM_DK), lambda n: (0, n, 0, 0, 0))
             if first_call else c_in)
    small = _mlstm_state_specs(lambda n: n)[1:]
    in_specs = [pl.BlockSpec(memory_space=pltpu.SMEM),
                seq_blk(rows, Z1_COLS), seq_blk(rows, MLSTM_WIDTH), seq_blk(SUBLANES, t),
                pl.BlockSpec((N_MLSTM_HEADS, MLSTM_DV), lambda n: (0, 0)), c_in] + small
    args = [bias, zs, zts, zgts, g_mlstm, c0_all, n0, m0]
    aliases = {}
    if not first_call:
        in_specs.append(pl.BlockSpec(memory_space=pl.ANY))
        args.append(c_out_prev)
        aliases = {len(args) - 1: 1}
    return pl.pallas_call(
        functools.partial(_mlstm_sample_kernel, t=t, n_valid=n_valid, layer=layer, first_call=first_call),
        grid=(s,),
        in_specs=in_specs,
        out_specs=[seq_blk(rows, MLSTM_WIDTH), c_out] + small,
        out_shape=[jax.ShapeDtypeStruct((s, rows, MLSTM_WIDTH), jnp.float32),
                   jax.ShapeDtypeStruct((depth,) + c0_all.shape[1:], jnp.float32)] + _mlstm_state_shapes(s)[1:],
        input_output_aliases=aliases,
        scratch_shapes=_MLSTM_SCRATCH,
        compiler_params=_cparams(("arbitrary",)),
        name="mlstm_sample",
    )(*args)


def _conv_norm_act(cv, cb, g, b):
    cv = cv + cb
    mu = jnp.mean(cv, axis=-1, keepdims=True)
    d = cv - mu
    var = jnp.mean(d * d, axis=-1, keepdims=True)
    y = d * lax.rsqrt(var + NORM_EPS) * g + b
    return y * jax.nn.sigmoid(y)


def _conv_prompt_kernel(ga_ref, gb_ref, w_ref, cb_ref, g_ref, b_ref, fill_ref, o_ref, st_ref, ext_ref, cv_ref,
                        *, nt, nbp):
    tb, rsub = CONV_TB, CONV_RSUB
    i = pl.program_id(0)
    step = i % nt

    @pl.when(i < nbp)
    def _():
        @pl.when(step == 0)
        def _():
            ext_ref[0:CONV_HIST, :] = jnp.zeros((CONV_HIST, CONV_CH), jnp.float32)

        ext_ref[CONV_HIST:, :] = ga_ref[...] * jax.nn.sigmoid(gb_ref[...])
        off = CONV_HIST - (CONV_WIDTH - 1)
        span = rsub + CONV_HIST
        for lc in range(CONV_CH // LANES):
            cols = slice(lc * LANES, (lc + 1) * LANES)
            for r0 in range(0, tb, rsub):
                e = ext_ref[r0:r0 + span, cols]
                acc = None
                for s in range(SUBLANES):
                    zs = e if s == 0 else pltpu.roll(e, span - s, 0)
                    for a in range(CONV_HIST // SUBLANES + 1):
                        j = SUBLANES * a + s - off
                        if 0 <= j < CONV_WIDTH:
                            term = w_ref[j:j + 1, cols] * zs[SUBLANES * a:SUBLANES * a + rsub]
                            acc = term if acc is None else acc + term
                cv_ref[r0:r0 + rsub, cols] = acc
        o_ref[...] = _conv_norm_act(cv_ref[...], cb_ref[...], g_ref[...], b_ref[...]).astype(o_ref.dtype)

        @pl.when(step == nt - 1)
        def _():
            st_ref[0] = ext_ref[tb + off:tb + CONV_HIST, :]

        ext_ref[0:CONV_HIST, :] = ext_ref[tb:tb + CONV_HIST, :]

    @pl.when(i >= nbp)
    def _():
        o_ref[...] = fill_ref[...]


def _conv_prompt(ztail, conv_w, conv_b, g_conv, b_conv, c_sample, batch, seq):
    tb = CONV_TB
    nt = seq // tb
    nbp = batch * nt
    fill = _fill_rows(c_sample, tb)
    nbs = fill.shape[0] // tb
    cur = lambda i: jnp.minimum(i, nbp - 1)
    vec = pl.BlockSpec((1, CONV_CH), lambda i: (0, 0))
    row = lambda col: pl.BlockSpec((tb, CONV_CH), lambda i: (cur(i), col))
    return pl.pallas_call(
        functools.partial(_conv_prompt_kernel, nt=nt, nbp=nbp),
        grid=(nbp + nbs,),
        in_specs=[row(MLSTM_WIDTH // CONV_CH), row(MLSTM_WIDTH // CONV_CH + 1),
                  pl.BlockSpec((CONV_WIDTH, CONV_CH), lambda i: (0, 0)), vec, vec, vec,
                  pl.BlockSpec((tb, CONV_CH), lambda i: (jnp.maximum(i - nbp, 0), 0))],
        out_specs=[pl.BlockSpec((tb, CONV_CH), lambda i: (i, 0)),
                   pl.BlockSpec((1, CONV_WIDTH - 1, CONV_CH), lambda i: (cur(i) // nt, 0, 0))],
        out_shape=[jax.ShapeDtypeStruct((ztail.shape[0], CONV_CH), jnp.bfloat16),
                   jax.ShapeDtypeStruct((batch, CONV_WIDTH - 1, CONV_CH), jnp.float32)],
        scratch_shapes=[pltpu.VMEM((tb + CONV_HIST, CONV_CH), jnp.float32),
                        pltpu.VMEM((tb, CONV_CH), jnp.float32)],
        compiler_params=_cparams(("arbitrary",)),
        name="conv_prompt",
    )(ztail, ztail, conv_w, conv_b.reshape(1, -1), g_conv.reshape(1, -1), b_conv.reshape(1, -1), fill)


def _conv_sample_kernel(ga_ref, gb_ref, st_ref, w_ref, cb_ref, g_ref, b_ref, o_ref, sto_ref, ext_ref, *, n_new):
    hist = CONV_WIDTH - 1
    ext_ref[0:hist] = st_ref[...]
    ext_ref[hist:] = ga_ref[...] * jax.nn.sigmoid(gb_ref[...])
    for t in range(n_new):
        acc = w_ref[0:1, :] * ext_ref[t]
        for j in range(1, CONV_WIDTH):
            acc = acc + w_ref[j:j + 1, :] * ext_ref[t + j]
        o_ref[t] = _conv_norm_act(acc, cb_ref[...], g_ref[...], b_ref[...])
    sto_ref[...] = ext_ref[n_new:]


def _conv_sample(ga, gb, state, conv_w, conv_b, g_conv, b_conv):
    n_new, s, ch = ga.shape
    hist = CONV_WIDTH - 1
    full = lambda shape: pl.BlockSpec(shape, lambda i: (0,) * len(shape))
    return pl.pallas_call(
        functools.partial(_conv_sample_kernel, n_new=n_new),
        grid=(1,),
        in_specs=[full((n_new, s, ch)), full((n_new, s, ch)), full((hist, s, ch)),
                  full((CONV_WIDTH, ch)), full((1, ch)), full((1, ch)), full((1, ch))],
        out_specs=[full((n_new, s, ch)), full((hist, s, ch))],
        out_shape=[jax.ShapeDtypeStruct((n_new, s, ch), jnp.float32),
                   jax.ShapeDtypeStruct((hist, s, ch), jnp.float32)],
        scratch_shapes=[pltpu.VMEM((hist + n_new, s, ch), jnp.float32)],
        compiler_params=_cparams(("arbitrary",)),
        name="conv_sample",
    )(ga, gb, state, conv_w, conv_b.reshape(1, -1), g_conv.reshape(1, -1), b_conv.reshape(1, -1))


def _layer(l, xp, xs, h, batch, seq, s_batch, s_len, ck, cv, c0_all, c_out_prev, n0, m0, conv_state,
           weights, vectors, g_next):
    w_in_t, w_out, w_up, w_down = weights
    (sinks, b_ig, b_fg, g_mlstm, conv_w, conv_b, g_conv, b_conv, g_post_mix, g_pre_mlp, g_post_mlp) = vectors
    mp = batch * seq
    rows = SUBLANES
    assert s_len <= rows
    f32, bf16 = jnp.float32, jnp.bfloat16

    z1 = _matmul_fullk([h], w_in_t, l, n_cols=Z1_COLS, tn=2 * MXU_COLS, out_dtype=f32, b_transposed=True,
                       name="proj_in_head")
    zt = _matmul_fullk([h], w_in_t, l, n_cols=TAIL_COLS, col0=COL_OG, tn=2 * MXU_COLS, out_dtype=f32,
                       b_transposed=True, name="proj_in_tail")
    gates = z1[:, COL_GATE:COL_OG]
    zgt = gates.T
    bias = jnp.stack([b_ig, b_fg])

    def pad_rows(a):
        a = a.reshape(s_batch, s_len, a.shape[-1])
        return jnp.pad(a, ((0, 0), (0, rows - s_len), (0, 0)))

    zs, zts = pad_rows(z1[mp:]), pad_rows(zt[mp:])
    zgts = jnp.pad(gates[mp:].reshape(s_batch, s_len, 2 * N_MLSTM_HEADS).transpose(0, 2, 1),
                   ((0, 0), (0, 0), (0, MLSTM_T_SAMPLE - s_len)))
    unpad = lambda a: a[:, :s_len].reshape(s_batch * s_len, a.shape[-1])

    a_s, kwin_s, vwin_s = _attn_sample(zs, ck.reshape(s_batch, WINDOW, KV_WIDTH),
                                       cv.reshape(s_batch, WINDOW, KV_WIDTH), sinks, s_len)
    a_mix, kwin_p, vwin_p = _attn_prompt(z1, sinks, unpad(a_s), batch, seq)
    m0b = jnp.broadcast_to(m0[:, :, None], m0.shape + (LANES,))
    b_s, c_s_all, n_s, m_s = _mlstm_sample(zs, zts, zgts, bias, g_mlstm, c0_all, n0, m0b, s_len, l, c_out_prev)
    b_mix, c_p, n_p, m_p = _mlstm_prompt(z1, zt, zgt, bias, g_mlstm, unpad(b_s), batch, seq)
    tm = lambda a: a.reshape(s_batch, s_len, CONV_CH).transpose(1, 0, 2)
    ga_s = tm(zt[mp:, MLSTM_WIDTH:MLSTM_WIDTH + CONV_CH])
    gb_s = tm(zt[mp:, MLSTM_WIDTH + CONV_CH:])
    c_sm, st_s = _conv_sample(ga_s, gb_s, conv_state.transpose(1, 0, 2), conv_w, conv_b, g_conv, b_conv)
    c_mix, st_p = _conv_prompt(zt, conv_w, conv_b, g_conv, b_conv,
                               c_sm.transpose(1, 0, 2).reshape(s_batch * s_len, CONV_CH), batch, seq)

    y = _matmul_fullk([a_mix, b_mix, c_mix], w_out, l, n_cols=D_MODEL, tn=2 * MXU_COLS, out_dtype=f32,
                      name="proj_out")
    xp, xs, h2 = _resid(y, xp, xs, g_post_mix, g_pre_mlp)
    ff = _matmul_fullk([h2], w_up, l, n_cols=w_up.shape[2], tn=2 * MXU_COLS, out_dtype=bf16, relu2=True,
                       name="mlp_up")
    y2 = _matmul([ff], w_down, l, n_cols=D_MODEL, tn=4 * MXU_COLS, out_dtype=f32, name="mlp_down")
    xp, xs, h_next = _resid(y2, xp, xs, g_post_mlp, g_next)

    kv = lambda a, n: a.reshape(n, WINDOW, N_KV_HEADS, HEAD_DIM)
    prompt_state = (kv(kwin_p, batch), kv(vwin_p, batch), c_p, n_p, m_p[:, :, 0], st_p)
    sample_state = (kv(kwin_s, s_batch), kv(vwin_s, s_batch), n_s, m_s[:, :, 0], st_s.transpose(1, 0, 2))
    return xp, xs, h_next, prompt_state, sample_state, c_s_all


def kernel(x_prompt, x_sample, cache_win_k, cache_win_v, state_mlstm_C, state_mlstm_n, state_mlstm_m, state_conv,
           g_pre_mix, w_in, attn_sinks, b_igate, b_fgate, g_mlstm, conv_w, conv_b, g_conv, b_conv,
           w_out, g_post_mix, g_pre_mlp, w_up, w_down, g_post_mlp):
    batch, seq, d = x_prompt.shape
    s_batch, s_len, _ = x_sample.shape
    depth = w_in.shape[0]
    xp = x_prompt.reshape(batch * seq, d)
    xs = x_sample.reshape(s_batch * s_len, d)
    h = _norm(xp, xs, g_pre_mix[0])
    w_in_t = jnp.swapaxes(w_in, 1, 2)
    p_states, s_states, s_c = [], [], None
    for l in range(depth):
        vectors = (attn_sinks[l], b_igate[l], b_fgate[l], g_mlstm[l], conv_w[l], conv_b[l],
                   g_conv[l], b_conv[l], g_post_mix[l], g_pre_mlp[l], g_post_mlp[l])
        g_next = g_pre_mix[l + 1] if l + 1 < depth else None
        xp, xs, h, ps, ss, s_c = _layer(l, xp, xs, h, batch, seq, s_batch, s_len, cache_win_k[l], cache_win_v[l],
                                        state_mlstm_C, s_c, state_mlstm_n[l], state_mlstm_m[l], state_conv[l],
                                        (w_in_t, w_out, w_up, w_down), vectors, g_next)
        p_states.append(ps)
        s_states.append(ss)
    stack = lambda states, i: jnp.stack([st[i] for st in states])
    s_k, s_v, s_n, s_m, s_cv = (stack(s_states, i) for i in range(5))
    return ((xp.reshape(batch, seq, d), xs.reshape(s_batch, s_len, d))
            + tuple(stack(p_states, i) for i in range(6))
            + (s_k, s_v, s_c, s_n, s_m, s_cv))
```

```python
import functools
import math

import jax
import jax.numpy as jnp
from jax import lax
from jax.experimental import pallas as pl
from jax.experimental.pallas import tpu as pltpu

D_MODEL = 4096
HEAD_DIM = 64
N_Q_HEADS = 16
N_KV_HEADS = 2
Q_PER_KV = N_Q_HEADS // N_KV_HEADS
WINDOW = 128
ROPE_THETA = 10000.0
N_MLSTM_HEADS = 4
MLSTM_DV = 512
MLSTM_DK = 256
CONV_CH = 1024
CONV_WIDTH = 31
NORM_EPS = 1e-6
PAST_LEN = 16384

ATT_WIDTH = N_Q_HEADS * HEAD_DIM
KV_WIDTH = N_KV_HEADS * HEAD_DIM
MLSTM_WIDTH = N_MLSTM_HEADS * MLSTM_DV
COL_QA = 0
COL_KA = COL_QA + ATT_WIDTH
COL_VA = COL_KA + KV_WIDTH
COL_QM = COL_VA + KV_WIDTH
COL_KM = COL_QM + N_MLSTM_HEADS * MLSTM_DK
COL_VM = COL_KM + N_MLSTM_HEADS * MLSTM_DK
COL_GATE = COL_VM + MLSTM_WIDTH
COL_OG = COL_GATE + 2 * N_MLSTM_HEADS
TAIL_COLS = MLSTM_WIDTH + 2 * CONV_CH

LANES = 128
SUBLANES = 8
MXU_COLS = 256
Z1_COLS = 5632
MLSTM_T_PROMPT = 256
MLSTM_T_SAMPLE = 128
CONV_TB = 256
CONV_RSUB = 64
CONV_HIST = 32
VMEM_LIMIT = 56 * 1024 * 1024


def _largest_tile(total, target, multiple):
    best = None
    for t in range(multiple, min(total, target) + 1, multiple):
        if total % t == 0:
            best = t
    assert best is not None, (total, target, multiple)
    return best


def _cparams(sem):
    return pltpu.CompilerParams(dimension_semantics=sem, vmem_limit_bytes=VMEM_LIMIT)


def _rms(x, g):
    return x * lax.rsqrt(jnp.mean(x * x, axis=-1, keepdims=True) + NORM_EPS) * g


def _row_tile(mp, ms):
    return _largest_tile(math.gcd(mp, ms), 256, 16)


def _split_specs(tr, d, nbp):
    prompt = pl.BlockSpec((tr, d), lambda i: (jnp.minimum(i, nbp - 1), 0))
    sample = pl.BlockSpec((tr, d), lambda i: (jnp.maximum(i - nbp, 0), 0))
    return prompt, sample


def _norm_kernel(xp_ref, xs_ref, g_ref, h_ref, *, nbp):
    i = pl.program_id(0)

    @pl.when(i < nbp)
    def _():
        h_ref[...] = _rms(xp_ref[...], g_ref[...]).astype(h_ref.dtype)

    @pl.when(i >= nbp)
    def _():
        h_ref[...] = _rms(xs_ref[...], g_ref[...]).astype(h_ref.dtype)


def _norm(xp, xs, g):
    (mp, d), ms = xp.shape, xs.shape[0]
    tr = _row_tile(mp, ms)
    nbp = mp // tr
    prompt, sample = _split_specs(tr, d, nbp)
    return pl.pallas_call(
        functools.partial(_norm_kernel, nbp=nbp),
        grid=((mp + ms) // tr,),
        in_specs=[prompt, sample, pl.BlockSpec((1, d), lambda i: (0, 0))],
        out_specs=pl.BlockSpec((tr, d), lambda i: (i, 0)),
        out_shape=jax.ShapeDtypeStruct((mp + ms, d), jnp.bfloat16),
        compiler_params=_cparams(("arbitrary",)),
        name="rmsnorm",
    )(xp, xs, g.reshape(1, d))


def _resid_kernel(y_ref, xp_ref, xs_ref, gpost_ref, *rest, nbp, with_next):
    if with_next:
        gnext_ref, xop_ref, xos_ref, h_ref = rest
    else:
        xop_ref, xos_ref = rest
    i = pl.program_id(0)

    def update(x_ref, xo_ref):
        xn = x_ref[...] + _rms(y_ref[...], gpost_ref[...])
        xo_ref[...] = xn
        if with_next:
            h_ref[...] = _rms(xn, gnext_ref[...]).astype(h_ref.dtype)

    pl.when(i < nbp)(lambda: update(xp_ref, xop_ref))
    pl.when(i >= nbp)(lambda: update(xs_ref, xos_ref))


def _resid(y, xp, xs, g_post, g_next):
    (mp, d), ms = xp.shape, xs.shape[0]
    tr = _row_tile(mp, ms)
    nbp = mp // tr
    prompt, sample = _split_specs(tr, d, nbp)
    row = pl.BlockSpec((tr, d), lambda i: (i, 0))
    vec = pl.BlockSpec((1, d), lambda i: (0, 0))
    with_next = g_next is not None
    out_specs = [prompt, sample] + ([row] if with_next else [])
    out_shape = [jax.ShapeDtypeStruct((mp, d), jnp.float32), jax.ShapeDtypeStruct((ms, d), jnp.float32)]
    args = [y, xp, xs, g_post.reshape(1, d)]
    if with_next:
        out_shape.append(jax.ShapeDtypeStruct((mp + ms, d), jnp.bfloat16))
        args.append(g_next.reshape(1, d))
    out = pl.pallas_call(
        functools.partial(_resid_kernel, nbp=nbp, with_next=with_next),
        grid=((mp + ms) // tr,),
        in_specs=[row, prompt, sample, vec] + ([vec] if with_next else []),
        out_specs=out_specs, out_shape=out_shape,
        compiler_params=_cparams(("arbitrary",)),
        name="resid_norm" if with_next else "resid_last",
    )(*args)
    return (out[0], out[1], out[2]) if with_next else (out[0], out[1], None)


def _mm_kernel(*refs, n_a, k_bounds, nk, relu2):
    a_refs, b_ref, o_ref, scratch = refs[:n_a], refs[n_a], refs[n_a + 1], refs[n_a + 2:]
    acc_ref = scratch[0] if scratch else o_ref
    k = pl.program_id(2)

    for t, a_ref in enumerate(a_refs):
        lo, hi = k_bounds[t], k_bounds[t + 1]

        def part(a_ref=a_ref):
            return jnp.dot(a_ref[...], b_ref[...].astype(jnp.bfloat16), preferred_element_type=jnp.float32)

        def assign(part=part):
            acc_ref[...] = part()

        def accumulate(part=part):
            acc_ref[...] += part()

        if lo == 0:
            pl.when(k == 0)(assign)
            if hi > 1:
                pl.when((k > 0) & (k < hi))(accumulate)
        else:
            pl.when((k >= lo) & (k < hi))(accumulate)

    if scratch or relu2:
        @pl.when(k == nk - 1)
        def _():
            r = acc_ref[...]
            if relu2:
                r = jnp.square(jnp.maximum(r, 0.0))
            o_ref[...] = r.astype(o_ref.dtype)


def _matmul(a_list, b, layer, *, n_cols, tn, tk_target, out_dtype, relu2=False, name):
    m = a_list[0].shape[0]
    tm = _largest_tile(m, 2080, 16)
    tk = _largest_tile(math.gcd(*[a.shape[1] for a in a_list]), tk_target, LANES)
    k_bounds = [0]
    for a in a_list:
        k_bounds.append(k_bounds[-1] + a.shape[1] // tk)
    nk = k_bounds[-1]
    assert nk * tk == b.shape[1] and n_cols % tn == 0

    def a_spec(lo, hi):
        return pl.BlockSpec((tm, tk), lambda i, j, k: (i, jnp.clip(k - lo, 0, hi - lo - 1)))

    scratch = [] if out_dtype == jnp.float32 else [pltpu.VMEM((tm, tn), jnp.float32)]
    return pl.pallas_call(
        functools.partial(_mm_kernel, n_a=len(a_list), k_bounds=tuple(k_bounds), nk=nk, relu2=relu2),
        grid=(m // tm, n_cols // tn, nk),
        in_specs=[a_spec(k_bounds[t], k_bounds[t + 1]) for t in range(len(a_list))]
                 + [pl.BlockSpec((None, tk, tn), lambda i, j, k: (layer, k, j))],
        out_specs=pl.BlockSpec((tm, tn), lambda i, j, k: (i, j)),
        out_shape=jax.ShapeDtypeStruct((m, n_cols), out_dtype),
        scratch_shapes=scratch,
        compiler_params=_cparams(("parallel", "parallel", "arbitrary")),
        name=name,
    )(*a_list, b)


def _mm_fullk_kernel(*refs, n_a, relu2, b_transposed, seg_starts):
    a_refs, b_ref, o_refs = refs[:n_a], refs[n_a], refs[n_a + 1:]
    acc, lo = None, 0
    for a_ref in a_refs:
        width = a_ref.shape[1]
        if b_transposed:
            part = lax.dot_general(a_ref[...], b_ref[:, lo:lo + width].astype(jnp.bfloat16),
                                   (((1,), (1,)), ((), ())), preferred_element_type=jnp.float32)
        else:
            part = jnp.dot(a_ref[...], b_ref[lo:lo + width, :].astype(jnp.bfloat16),
                           preferred_element_type=jnp.float32)
        acc = part if acc is None else acc + part
        lo += width
    if relu2:
        acc = jnp.square(jnp.maximum(acc, 0.0))
    if len(o_refs) == 1:
        o_refs[0][...] = acc.astype(o_refs[0].dtype)
    else:
        j = pl.program_id(1)
        for t, o_ref in enumerate(o_refs):
            def store(o_ref=o_ref):
                o_ref[...] = acc.astype(o_ref.dtype)
            pl.when((j >= seg_starts[t]) & (j < seg_starts[t + 1]))(store)


def _matmul_fullk(a_list, b, layer, *, segments, tn, out_dtype, relu2=False, b_transposed=False,
                  tm_target=2080, a_buffers=1, name):
    m = a_list[0].shape[0]
    kdim = sum(a.shape[1] for a in a_list)
    tm = _largest_tile(m, tm_target, 16)
    seg_starts = [0]
    for col0, n_cols in segments:
        assert n_cols % tn == 0
        seg_starts.append(seg_starts[-1] + n_cols // tn)

    def first_col(j):
        col = segments[0][0] + j * tn
        for t in range(1, len(segments)):
            col = jnp.where(j >= seg_starts[t], segments[t][0] + (j - seg_starts[t]) * tn, col)
        return col

    if b_transposed:
        assert kdim == b.shape[2] and all(c % SUBLANES == 0 for c, _ in segments)
        b_spec = pl.BlockSpec((None, pl.Element(tn), pl.Element(kdim)),
                              lambda i, j: (layer, pl.multiple_of(first_col(j), SUBLANES), 0))
    else:
        assert kdim == b.shape[1] and all(c % tn == 0 for c, _ in segments)
        b_spec = pl.BlockSpec((None, kdim, tn), lambda i, j: (layer, 0, first_col(j) // tn))

    def out_spec(t):
        lo, n = seg_starts[t], seg_starts[t + 1] - seg_starts[t]
        return pl.BlockSpec((tm, tn), lambda i, j: (i, jnp.clip(j - lo, 0, n - 1)))

    a_mode = dict(pipeline_mode=pl.Buffered(1)) if a_buffers == 1 else {}
    outs = pl.pallas_call(
        functools.partial(_mm_fullk_kernel, n_a=len(a_list), relu2=relu2, b_transposed=b_transposed,
                          seg_starts=tuple(seg_starts)),
        grid=(m // tm, seg_starts[-1]),
        in_specs=[pl.BlockSpec((tm, a.shape[1]), lambda i, j: (i, 0), **a_mode) for a in a_list] + [b_spec],
        out_specs=[out_spec(t) for t in range(len(segments))],
        out_shape=[jax.ShapeDtypeStruct((m, n_cols), out_dtype) for _, n_cols in segments],
        compiler_params=_cparams(("parallel", "arbitrary")),
        name=name,
    )(*a_list, b)
    return outs[0] if len(segments) == 1 else outs


def _rope_tables(pos0, length):
    half = HEAD_DIM // 2
    inv_freq = ROPE_THETA ** (-jnp.arange(half, dtype=jnp.float32) / half)
    pos = (pos0 + jnp.arange(length, dtype=jnp.int32)).astype(jnp.float32)
    ang = pos[:, None] * inv_freq[None, :]
    cos, sin = jnp.cos(ang), jnp.sin(ang)
    reps = LANES // HEAD_DIM
    cos_t = jnp.tile(jnp.concatenate([cos, cos], axis=-1), (1, reps))
    sin_t = jnp.tile(jnp.concatenate([-sin, sin], axis=-1), (1, reps))
    return cos_t, sin_t


def _rope(x, cos_t, sin_t):
    half = HEAD_DIM // 2
    lane = lax.broadcasted_iota(jnp.int32, x.shape, 1)
    swapped = jnp.where((lane % HEAD_DIM) < half,
                        pltpu.roll(x, LANES - half, 1), pltpu.roll(x, half, 1))
    return x * cos_t + swapped * sin_t


def _rope_wide(x, cos_t, sin_t):
    return jnp.concatenate(
        [_rope(x[:, c * LANES:(c + 1) * LANES], cos_t, sin_t) for c in range(x.shape[1] // LANES)], axis=-1)


def _attend(q_rot, k_ext, v_ext, mask_fn, sink_ref):
    rq, s_len = q_rot.shape[0], k_ext.shape[0]
    kb = k_ext.astype(jnp.bfloat16)
    vb = v_ext.astype(jnp.bfloat16)
    scale = HEAD_DIM ** -0.5
    stack = Q_PER_KV
    rows = lax.broadcasted_iota(jnp.int32, (stack * rq, s_len), 0)
    mask = mask_fn(rows % rq, lax.broadcasted_iota(jnp.int32, (stack * rq, s_len), 1))
    outs = []
    for g in range(N_KV_HEADS):
        kg = kb[:, g * HEAD_DIM:(g + 1) * HEAD_DIM]
        vg = vb[:, g * HEAD_DIM:(g + 1) * HEAD_DIM]
        heads = range(g * stack, (g + 1) * stack)
        qg = jnp.concatenate([q_rot[:, h * HEAD_DIM:(h + 1) * HEAD_DIM] for h in heads], axis=0)
        sink = jnp.concatenate([jnp.full((rq, 1), sink_ref[0, h], jnp.float32) for h in heads], axis=0)
        s = lax.dot_general((qg * scale).astype(jnp.bfloat16), kg, (((1,), (1,)), ((), ())),
                            preferred_element_type=jnp.float32)
        s = jnp.where(mask, s, -jnp.inf)
        m = jnp.maximum(jnp.max(s, axis=-1, keepdims=True), sink)
        p = jnp.exp(s - m)
        inv = 1.0 / (jnp.sum(p, axis=-1, keepdims=True) + jnp.exp(sink - m))
        o = jnp.dot(p.astype(jnp.bfloat16), vg, preferred_element_type=jnp.float32) * inv
        outs.extend(o[hh * rq:(hh + 1) * rq] for hh in range(stack))
    return jnp.concatenate(outs, axis=-1)


def _fill_rows(sample_rows, block_rows):
    ms = sample_rows.shape[0]
    padded = -(-ms // block_rows) * block_rows
    return jnp.pad(sample_rows.astype(jnp.bfloat16), ((0, padded - ms), (0, 0)))


def _attn_prompt_kernel(sink_ref, q_ref, kc_ref, kp_ref, vc_ref, vp_ref,
                        cosc_ref, sinc_ref, cosp_ref, sinp_ref, fill_ref,
                        o_ref, kwin_ref, vwin_ref, *, nb, nbp):
    i = pl.program_id(0)
    j = i % nb

    @pl.when(i < nbp)
    def _():
        cosc, sinc = cosc_ref[...], sinc_ref[...]
        q_rot = _rope_wide(q_ref[...], cosc, sinc)
        k_cur = _rope(kc_ref[...], cosc, sinc)
        k_prev = _rope(kp_ref[...], cosp_ref[...], sinp_ref[...])
        k_ext = jnp.concatenate([k_prev, k_cur], axis=0)
        v_ext = jnp.concatenate([vp_ref[...], vc_ref[...]], axis=0)

        def mask_fn(tq, sk):
            return ((sk >= tq) & (sk < WINDOW) & (j > 0)) | ((sk >= WINDOW) & ((sk - WINDOW) <= tq))

        o_ref[...] = _attend(q_rot, k_ext, v_ext, mask_fn, sink_ref).astype(o_ref.dtype)

        @pl.when(j == nb - 1)
        def _():
            kwin_ref[0] = k_cur
            vwin_ref[0] = vc_ref[...]

    @pl.when(i >= nbp)
    def _():
        o_ref[...] = fill_ref[...]


def _attn_prompt(z1, sinks, a_sample, batch, seq):
    nb = seq // WINDOW
    nbp = batch * nb
    fill = _fill_rows(a_sample, WINDOW)
    nbs = fill.shape[0] // WINDOW
    cos_t, sin_t = _rope_tables(0, seq)
    cur = lambda i: jnp.minimum(i, nbp - 1)
    prev = lambda i: cur(i) - (cur(i) % nb > 0)
    blk = lambda col: pl.BlockSpec((WINDOW, LANES), lambda i: (cur(i), col))
    blk_prev = lambda col: pl.BlockSpec((WINDOW, LANES), lambda i: (prev(i), col))
    tab = pl.BlockSpec((WINDOW, LANES), lambda i: (cur(i) % nb, 0))
    tab_prev = pl.BlockSpec((WINDOW, LANES), lambda i: (prev(i) % nb, 0))
    win = pl.BlockSpec((1, WINDOW, KV_WIDTH), lambda i: (cur(i) // nb, 0, 0))
    return pl.pallas_call(
        functools.partial(_attn_prompt_kernel, nb=nb, nbp=nbp),
        grid=(nbp + nbs,),
        in_specs=[pl.BlockSpec(memory_space=pltpu.SMEM),
                  pl.BlockSpec((WINDOW, ATT_WIDTH), lambda i: (cur(i), 0)),
                  blk(COL_KA // LANES), blk_prev(COL_KA // LANES),
                  blk(COL_VA // LANES), blk_prev(COL_VA // LANES),
                  tab, tab, tab_prev, tab_prev,
                  pl.BlockSpec((WINDOW, ATT_WIDTH), lambda i: (jnp.maximum(i - nbp, 0), 0))],
        out_specs=[pl.BlockSpec((WINDOW, ATT_WIDTH), lambda i: (i, 0)), win, win],
        out_shape=[jax.ShapeDtypeStruct((z1.shape[0], ATT_WIDTH), jnp.bfloat16),
                   jax.ShapeDtypeStruct((batch, WINDOW, KV_WIDTH), jnp.float32),
                   jax.ShapeDtypeStruct((batch, WINDOW, KV_WIDTH), jnp.float32)],
        compiler_params=_cparams(("arbitrary",)),
        name="attn_prompt",
    )(sinks.reshape(1, N_Q_HEADS), z1, z1, z1, z1, z1, cos_t, sin_t, cos_t, sin_t, fill)


def _attn_sample_kernel(sink_ref, z_ref, ck_ref, cv_ref, cos_ref, sin_ref,
                        o_ref, kout_ref, vout_ref, kext_ref, vext_ref, *, n_new, rows):
    cos_t, sin_t = cos_ref[...], sin_ref[...]
    z = z_ref[0]
    q_rot = _rope_wide(z[:, COL_QA:COL_QA + ATT_WIDTH], cos_t, sin_t)
    k_new = _rope(z[:, COL_KA:COL_KA + KV_WIDTH], cos_t, sin_t)
    v_new = z[:, COL_VA:COL_VA + KV_WIDTH]
    kext_ref[0:WINDOW, :] = ck_ref[0]
    vext_ref[0:WINDOW, :] = cv_ref[0]
    kext_ref[WINDOW:WINDOW + rows, :] = k_new
    vext_ref[WINDOW:WINDOW + rows, :] = v_new
    fill = jnp.zeros((WINDOW - rows, KV_WIDTH), jnp.float32)
    kext_ref[WINDOW + rows:, :] = fill
    vext_ref[WINDOW + rows:, :] = fill

    def mask_fn(tq, sk):
        return ((sk >= tq) & (sk < WINDOW)) | ((sk >= WINDOW) & ((sk - WINDOW) <= tq) & (sk < WINDOW + n_new))

    o_ref[0] = _attend(q_rot, kext_ref[...], vext_ref[...], mask_fn, sink_ref)
    kout_ref[0] = kext_ref[pl.ds(n_new, WINDOW), :]
    vout_ref[0] = vext_ref[pl.ds(n_new, WINDOW), :]


def _attn_sample(zs, cache_k, cache_v, sinks, n_new):
    s, rows, _ = zs.shape
    cos_t, sin_t = _rope_tables(PAST_LEN, rows)
    seq_blk = lambda r, c: pl.BlockSpec((1, r, c), lambda n: (n, 0, 0))
    tab = pl.BlockSpec((rows, LANES), lambda n: (0, 0))
    return pl.pallas_call(
        functools.partial(_attn_sample_kernel, n_new=n_new, rows=rows),
        grid=(s,),
        in_specs=[pl.BlockSpec(memory_space=pltpu.SMEM),
                  seq_blk(rows, Z1_COLS), seq_blk(WINDOW, KV_WIDTH), seq_blk(WINDOW, KV_WIDTH), tab, tab],
        out_specs=[seq_blk(rows, ATT_WIDTH), seq_blk(WINDOW, KV_WIDTH), seq_blk(WINDOW, KV_WIDTH)],
        out_shape=[jax.ShapeDtypeStruct((s, rows, ATT_WIDTH), jnp.float32),
                   jax.ShapeDtypeStruct((s, WINDOW, KV_WIDTH), jnp.float32),
                   jax.ShapeDtypeStruct((s, WINDOW, KV_WIDTH), jnp.float32)],
        scratch_shapes=[pltpu.VMEM((2 * WINDOW, KV_WIDTH), jnp.float32),
                        pltpu.VMEM((2 * WINDOW, KV_WIDTH), jnp.float32)],
        compiler_params=_cparams(("parallel",)),
        name="attn_sample",
    )(sinks.reshape(1, N_Q_HEADS), zs, cache_k, cache_v, cos_t, sin_t)


def _log_sigmoid(x):
    return jnp.minimum(x, 0.0) - jnp.log(1.0 + jnp.exp(-jnp.abs(x)))


def _mlstm_chunk(q, k, v, li_col, lf_col, li_row, lf_row, c_state, n_state, m_state):
    t = q.shape[0]
    r = lax.broadcasted_iota(jnp.int32, (t, t), 0)
    c = lax.broadcasted_iota(jnp.int32, (t, t), 1)
    tril = c <= r
    b_col = jnp.sum(jnp.where(tril, lf_row, 0.0), axis=1, keepdims=True)
    b_row = jnp.sum(jnp.where(r <= c, lf_col, 0.0), axis=0, keepdims=True)
    b_last = jnp.sum(lf_row, axis=1, keepdims=True)
    log_d = jnp.where(tril, b_col - b_row + li_row, -jnp.inf)
    inter = b_col + m_state
    m_row = jnp.maximum(inter, jnp.max(log_d, axis=1, keepdims=True))
    qb = q.astype(jnp.bfloat16)
    kb = k.astype(jnp.bfloat16)
    vb = v.astype(jnp.bfloat16)
    qk = lax.dot_general(qb, kb, (((1,), (1,)), ((), ())), preferred_element_type=jnp.float32)
    s = qk * jnp.exp(log_d - m_row)
    w_inter = jnp.exp(inter - m_row)
    cq = lax.dot_general(qb, c_state.astype(jnp.bfloat16), (((1,), (1,)), ((), ())),
                         preferred_element_type=jnp.float32)
    num = jnp.dot(s.astype(jnp.bfloat16), vb, preferred_element_type=jnp.float32) + w_inter * cq
    den = jnp.sum(s, axis=1, keepdims=True) + w_inter * jnp.sum(q * n_state, axis=1, keepdims=True)
    h = num / jnp.maximum(jnp.abs(den), jnp.exp(-m_row))
    g_row = b_last - b_row + li_row
    g_col = b_last - b_col + li_col
    m_new = jnp.maximum(b_last + m_state, jnp.max(g_row, axis=1, keepdims=True))
    decay = jnp.exp(b_last + m_state - m_new)
    wk_col = jnp.exp(g_col - m_new)
    wv_t = (wk_col * v).T.astype(jnp.bfloat16)
    c_new = decay * c_state + jnp.dot(wv_t, kb, preferred_element_type=jnp.float32)
    n_new = decay * n_state + jnp.sum(wk_col * k, axis=0, keepdims=True)
    return h, c_new, n_new, m_new


def _mlstm_heads(bias_ref, z, og, zgt, gm_ref, c_s, n_s, m_s, write_out, *, t, n_valid):
    rows = z.shape[0]
    zg = z[:, COL_GATE:COL_GATE + LANES]
    if rows < t:
        pad = lambda a: jnp.concatenate([a, jnp.zeros((t - rows, a.shape[1]), a.dtype)], axis=0)
        z, og, zg = pad(z), pad(og), pad(zg)
    valid_col = lax.broadcasted_iota(jnp.int32, (t, 1), 0) < n_valid
    valid_row = lax.broadcasted_iota(jnp.int32, (1, t), 1) < n_valid
    for h in range(N_MLSTM_HEADS):
        q = z[:, COL_QM + h * MLSTM_DK:COL_QM + (h + 1) * MLSTM_DK]
        k = z[:, COL_KM + h * MLSTM_DK:COL_KM + (h + 1) * MLSTM_DK] * (MLSTM_DK ** -0.5)
        v = z[:, COL_VM + h * MLSTM_DV:COL_VM + (h + 1) * MLSTM_DV]
        b_i = bias_ref[0, h]
        b_f = bias_ref[1, h]
        li_col = zg[:, h:h + 1] + b_i
        lf_col = _log_sigmoid(zg[:, N_MLSTM_HEADS + h:N_MLSTM_HEADS + h + 1] + b_f)
        li_row = zgt[h:h + 1, :] + b_i
        lf_row = _log_sigmoid(zgt[N_MLSTM_HEADS + h:N_MLSTM_HEADS + h + 1, :] + b_f)
        if n_valid < t:
            li_col = jnp.where(valid_col, li_col, -jnp.inf)
            lf_col = jnp.where(valid_col, lf_col, 0.0)
            li_row = jnp.where(valid_row, li_row, -jnp.inf)
            lf_row = jnp.where(valid_row, lf_row, 0.0)
        hm, c_new, n_new, m_new = _mlstm_chunk(
            q, k, v, li_col, lf_col, li_row, lf_row, c_s[h], n_s[h:h + 1, :], m_s[h:h + 1, 0:1])
        c_s[h] = c_new
        n_s[h:h + 1, :] = n_new
        m_s[h:h + 1, :] = jnp.broadcast_to(m_new, (1, LANES))
        gate = jax.nn.sigmoid(og[:, h * MLSTM_DV:(h + 1) * MLSTM_DV])
        write_out(h, (_rms(hm, gm_ref[h:h + 1, :]) * gate)[:rows])


def _mlstm_prompt_kernel(bias_ref, z_ref, og_ref, zgt_ref, gm_ref, fill_ref,
                         o_ref, co_ref, no_ref, mo_ref, c_s, n_s, m_s, *, t, nc, nbp):
    i = pl.program_id(0)
    step = i % nc

    @pl.when(i < nbp)
    def _():
        @pl.when(step == 0)
        def _():
            c_s[...] = jnp.zeros_like(c_s)
            n_s[...] = jnp.zeros_like(n_s)
            m_s[...] = jnp.zeros_like(m_s)

        def write_out(h, out):
            o_ref[:, h * MLSTM_DV:(h + 1) * MLSTM_DV] = out.astype(o_ref.dtype)

        _mlstm_heads(bias_ref, z_ref[...], og_ref[...], zgt_ref[...], gm_ref, c_s, n_s, m_s, write_out,
                     t=t, n_valid=t)

        @pl.when(step == nc - 1)
        def _():
            co_ref[0] = c_s[...]
            no_ref[0] = n_s[...]
            mo_ref[0] = m_s[...]

    @pl.when(i >= nbp)
    def _():
        o_ref[...] = fill_ref[...]


def _mlstm_sample_kernel(bias_ref, z_ref, og_ref, zgt_ref, gm_ref, c0_ref, n0_ref, m0_ref, *rest,
                         t, n_valid, layer, first_call):
    o_ref, co_ref, no_ref, mo_ref, c_s, n_s, m_s = rest if first_call else rest[1:]
    c_s[...] = c0_ref[0]
    n_s[...] = n0_ref[0]
    m_s[...] = m0_ref[0]

    def write_out(h, out):
        o_ref[0, :, h * MLSTM_DV:(h + 1) * MLSTM_DV] = out

    _mlstm_heads(bias_ref, z_ref[0], og_ref[0], zgt_ref[0], gm_ref, c_s, n_s, m_s, write_out,
                 t=t, n_valid=n_valid)
    if first_call:
        for other in range(co_ref.shape[0]):
            co_ref[other, 0] = c_s[...] if other == layer else jnp.zeros_like(c_s)
    else:
        co_ref[0] = c_s[...]
    no_ref[0] = n_s[...]
    mo_ref[0] = m_s[...]


_MLSTM_SCRATCH = [pltpu.VMEM((N_MLSTM_HEADS, MLSTM_DV, MLSTM_DK), jnp.float32),
                  pltpu.VMEM((N_MLSTM_HEADS, MLSTM_DK), jnp.float32),
                  pltpu.VMEM((N_MLSTM_HEADS, LANES), jnp.float32)]


def _mlstm_state_shapes(n):
    return [jax.ShapeDtypeStruct((n, N_MLSTM_HEADS, MLSTM_DV, MLSTM_DK), jnp.float32),
            jax.ShapeDtypeStruct((n, N_MLSTM_HEADS, MLSTM_DK), jnp.float32),
            jax.ShapeDtypeStruct((n, N_MLSTM_HEADS, LANES), jnp.float32)]


def _mlstm_state_specs(index):
    return [pl.BlockSpec((1, N_MLSTM_HEADS, MLSTM_DV, MLSTM_DK), lambda i: (index(i), 0, 0, 0)),
            pl.BlockSpec((1, N_MLSTM_HEADS, MLSTM_DK), lambda i: (index(i), 0, 0)),
            pl.BlockSpec((1, N_MLSTM_HEADS, LANES), lambda i: (index(i), 0, 0))]


def _mlstm_prompt(z1, ztail, zgt, bias, g_mlstm, b_sample, batch, seq):
    t = MLSTM_T_PROMPT
    nc = seq // t
    nbp = batch * nc
    fill = _fill_rows(b_sample, t)
    nbs = fill.shape[0] // t
    cur = lambda i: jnp.minimum(i, nbp - 1)
    row = lambda w: pl.BlockSpec((t, w), lambda i: (cur(i), 0))
    return pl.pallas_call(
        functools.partial(_mlstm_prompt_kernel, t=t, nc=nc, nbp=nbp),
        grid=(nbp + nbs,),
        in_specs=[pl.BlockSpec(memory_space=pltpu.SMEM),
                  row(Z1_COLS), row(MLSTM_WIDTH),
                  pl.BlockSpec((SUBLANES, t), lambda i: (0, cur(i))),
                  pl.BlockSpec((N_MLSTM_HEADS, MLSTM_DV), lambda i: (0, 0)),
                  pl.BlockSpec((t, MLSTM_WIDTH), lambda i: (jnp.maximum(i - nbp, 0), 0))],
        out_specs=[pl.BlockSpec((t, MLSTM_WIDTH), lambda i: (i, 0))] + _mlstm_state_specs(lambda i: cur(i) // nc),
        out_shape=[jax.ShapeDtypeStruct((z1.shape[0], MLSTM_WIDTH), jnp.bfloat16)] + _mlstm_state_shapes(batch),
        scratch_shapes=_MLSTM_SCRATCH,
        compiler_params=_cparams(("arbitrary",)),
        name="mlstm_prompt",
    )(bias, z1, ztail, zgt, g_mlstm, fill)


def _mlstm_sample(zs, zts, zgts, bias, g_mlstm, c0_all, n0, m0, n_valid, layer, c_out_prev):
    s, rows, _ = zs.shape
    t = MLSTM_T_SAMPLE
    depth = c0_all.shape[0]
    first_call = c_out_prev is None
    seq_blk = lambda r, c: pl.BlockSpec((1, r, c), lambda n: (n, 0, 0))
    c_in = pl.BlockSpec((None, 1, N_MLSTM_HEADS, MLSTM_DV, MLSTM_DK), lambda n: (layer, n, 0, 0, 0))
    c_out = (pl.BlockSpec((depth, 1, N_MLSTM_HEADS, MLSTM_DV, MLSTM_DK), lambda n: (0, n, 0, 0, 0))
             if first_call else c_in)
    small = _mlstm_state_specs(lambda n: n)[1:]
    in_specs = [pl.BlockSpec(memory_space=pltpu.SMEM),
                seq_blk(rows, Z1_COLS), seq_blk(rows, MLSTM_WIDTH), seq_blk(SUBLANES, t),
                pl.BlockSpec((N_MLSTM_HEADS, MLSTM_DV), lambda n: (0, 0)), c_in] + small
    args = [bias, zs, zts, zgts, g_mlstm, c0_all, n0, m0]
    aliases = {}
    if not first_call:
        in_specs.append(pl.BlockSpec(memory_space=pl.ANY))
        args.append(c_out_prev)
        aliases = {len(args) - 1: 1}
    return pl.pallas_call(
        functools.partial(_mlstm_sample_kernel, t=t, n_valid=n_valid, layer=layer, first_call=first_call),
        grid=(s,),
        in_specs=in_specs,
        out_specs=[seq_blk(rows, MLSTM_WIDTH), c_out] + small,
        out_shape=[jax.ShapeDtypeStruct((s, rows, MLSTM_WIDTH), jnp.float32),
                   jax.ShapeDtypeStruct((depth,) + c0_all.shape[1:], jnp.float32)] + _mlstm_state_shapes(s)[1:],
        input_output_aliases=aliases,
        scratch_shapes=_MLSTM_SCRATCH,
        compiler_params=_cparams(("arbitrary",)),
        name="mlstm_sample",
    )(*args)


def _conv_norm_act(cv, cb, g, b):
    cv = cv + cb
    mu = jnp.mean(cv, axis=-1, keepdims=True)
    d = cv - mu
    var = jnp.mean(d * d, axis=-1, keepdims=True)
    y = d * lax.rsqrt(var + NORM_EPS) * g + b
    return y * jax.nn.sigmoid(y)


def _conv_prompt_kernel(ga_ref, gb_ref, w_ref, cb_ref, g_ref, b_ref, fill_ref, o_ref, st_ref, ext_ref, cv_ref,
                        *, nt, nbp):
    tb, rsub = CONV_TB, CONV_RSUB
    i = pl.program_id(0)
    step = i % nt

    @pl.when(i < nbp)
    def _():
        @pl.when(step == 0)
        def _():
            ext_ref[0:CONV_HIST, :] = jnp.zeros((CONV_HIST, CONV_CH), jnp.float32)

        ext_ref[CONV_HIST:, :] = ga_ref[...] * jax.nn.sigmoid(gb_ref[...])
        off = CONV_HIST - (CONV_WIDTH - 1)
        span = rsub + CONV_HIST
        for lc in range(CONV_CH // LANES):
            cols = slice(lc * LANES, (lc + 1) * LANES)
            for r0 in range(0, tb, rsub):
                e = ext_ref[r0:r0 + span, cols]
                acc = None
                for s in range(SUBLANES):
                    zs = e if s == 0 else pltpu.roll(e, span - s, 0)
                    for a in range(CONV_HIST // SUBLANES + 1):
                        j = SUBLANES * a + s - off
                        if 0 <= j < CONV_WIDTH:
                            term = w_ref[j:j + 1, cols] * zs[SUBLANES * a:SUBLANES * a + rsub]
                            acc = term if acc is None else acc + term
                cv_ref[r0:r0 + rsub, cols] = acc
        o_ref[...] = _conv_norm_act(cv_ref[...], cb_ref[...], g_ref[...], b_ref[...]).astype(o_ref.dtype)

        @pl.when(step == nt - 1)
        def _():
            st_ref[0] = ext_ref[tb + off:tb + CONV_HIST, :]

        ext_ref[0:CONV_HIST, :] = ext_ref[tb:tb + CONV_HIST, :]

    @pl.when(i >= nbp)
    def _():
        o_ref[...] = fill_ref[...]


def _conv_prompt(ztail, conv_w, conv_b, g_conv, b_conv, c_sample, batch, seq):
    tb = CONV_TB
    nt = seq // tb
    nbp = batch * nt
    fill = _fill_rows(c_sample, tb)
    nbs = fill.shape[0] // tb
    cur = lambda i: jnp.minimum(i, nbp - 1)
    vec = pl.BlockSpec((1, CONV_CH), lambda i: (0, 0))
    row = lambda col: pl.BlockSpec((tb, CONV_CH), lambda i: (cur(i), col))
    return pl.pallas_call(
        functools.partial(_conv_prompt_kernel, nt=nt, nbp=nbp),
        grid=(nbp + nbs,),
        in_specs=[row(MLSTM_WIDTH // CONV_CH), row(MLSTM_WIDTH // CONV_CH + 1),
                  pl.BlockSpec((CONV_WIDTH, CONV_CH), lambda i: (0, 0)), vec, vec, vec,
                  pl.BlockSpec((tb, CONV_CH), lambda i: (jnp.maximum(i - nbp, 0), 0))],
        out_specs=[pl.BlockSpec((tb, CONV_CH), lambda i: (i, 0)),
                   pl.BlockSpec((1, CONV_WIDTH - 1, CONV_CH), lambda i: (cur(i) // nt, 0, 0))],
        out_shape=[jax.ShapeDtypeStruct((ztail.shape[0], CONV_CH), jnp.bfloat16),
                   jax.ShapeDtypeStruct((batch, CONV_WIDTH - 1, CONV_CH), jnp.float32)],
        scratch_shapes=[pltpu.VMEM((tb + CONV_HIST, CONV_CH), jnp.float32),
                        pltpu.VMEM((tb, CONV_CH), jnp.float32)],
        compiler_params=_cparams(("arbitrary",)),
        name="conv_prompt",
    )(ztail, ztail, conv_w, conv_b.reshape(1, -1), g_conv.reshape(1, -1), b_conv.reshape(1, -1), fill)


def _conv_sample_kernel(ga_ref, gb_ref, st_ref, w_ref, cb_ref, g_ref, b_ref, o_ref, sto_ref, ext_ref, *, n_new):
    hist = CONV_WIDTH - 1
    ext_ref[0:hist] = st_ref[...]
    ext_ref[hist:] = ga_ref[...] * jax.nn.sigmoid(gb_ref[...])
    for t in range(n_new):
        acc = w_ref[0:1, :] * ext_ref[t]
        for j in range(1, CONV_WIDTH):
            acc = acc + w_ref[j:j + 1, :] * ext_ref[t + j]
        o_ref[t] = _conv_norm_act(acc, cb_ref[...], g_ref[...], b_ref[...])
    sto_ref[...] = ext_ref[n_new:]


def _conv_sample(ga, gb, state, conv_w, conv_b, g_conv, b_conv):
    n_new, s, ch = ga.shape
    hist = CONV_WIDTH - 1
    full = lambda shape: pl.BlockSpec(shape, lambda i: (0,) * len(shape))
    return pl.pallas_call(
        functools.partial(_conv_sample_kernel, n_new=n_new),
        grid=(1,),
        in_specs=[full((n_new, s, ch)), full((n_new, s, ch)), full((hist, s, ch)),
                  full((CONV_WIDTH, ch)), full((1, ch)), full((1, ch)), full((1, ch))],
        out_specs=[full((n_new, s, ch)), full((hist, s, ch))],
        out_shape=[jax.ShapeDtypeStruct((n_new, s, ch), jnp.float32),
                   jax.ShapeDtypeStruct((hist, s, ch), jnp.float32)],
        scratch_shapes=[pltpu.VMEM((hist + n_new, s, ch), jnp.float32)],
        compiler_params=_cparams(("arbitrary",)),
        name="conv_sample",
    )(ga, gb, state, conv_w, conv_b.reshape(1, -1), g_conv.reshape(1, -1), b_conv.reshape(1, -1))


def _layer(l, xp, xs, h, batch, seq, s_batch, s_len, ck, cv, c0_all, c_out_prev, n0, m0, conv_state,
           weights, vectors, g_next):
    w_in_t, w_out, w_up, w_down = weights
    (sinks, b_ig, b_fg, g_mlstm, conv_w, conv_b, g_conv, b_conv, g_post_mix, g_pre_mlp, g_post_mlp) = vectors
    mp = batch * seq
    rows = SUBLANES
    assert s_len <= rows
    f32, bf16 = jnp.float32, jnp.bfloat16

    z1, zt = _matmul_fullk([h], w_in_t, l, segments=[(0, Z1_COLS), (COL_OG, TAIL_COLS)], tn=2 * MXU_COLS,
                           out_dtype=f32, b_transposed=True, name="proj_in")
    gates = z1[:, COL_GATE:COL_OG]
    zgt = gates.T
    bias = jnp.stack([b_ig, b_fg])

    def pad_rows(a):
        a = a.reshape(s_batch, s_len, a.shape[-1])
        return jnp.pad(a, ((0, 0), (0, rows - s_len), (0, 0)))

    zs, zts = pad_rows(z1[mp:]), pad_rows(zt[mp:])
    zgts = jnp.pad(gates[mp:].reshape(s_batch, s_len, 2 * N_MLSTM_HEADS).transpose(0, 2, 1),
                   ((0, 0), (0, 0), (0, MLSTM_T_SAMPLE - s_len)))
    unpad = lambda a: a[:, :s_len].reshape(s_batch * s_len, a.shape[-1])

    a_s, kwin_s, vwin_s = _attn_sample(zs, ck.reshape(s_batch, WINDOW, KV_WIDTH),
                                       cv.reshape(s_batch, WINDOW, KV_WIDTH), sinks, s_len)
    a_mix, kwin_p, vwin_p = _attn_prompt(z1, sinks, unpad(a_s), batch, seq)
    m0b = jnp.broadcast_to(m0[:, :, None], m0.shape + (LANES,))
    b_s, c_s_all, n_s, m_s = _mlstm_sample(zs, zts, zgts, bias, g_mlstm, c0_all, n0, m0b, s_len, l, c_out_prev)
    b_mix, c_p, n_p, m_p = _mlstm_prompt(z1, zt, zgt, bias, g_mlstm, unpad(b_s), batch, seq)
    tm = lambda a: a.reshape(s_batch, s_len, CONV_CH).transpose(1, 0, 2)
    ga_s = tm(zt[mp:, MLSTM_WIDTH:MLSTM_WIDTH + CONV_CH])
    gb_s = tm(zt[mp:, MLSTM_WIDTH + CONV_CH:])
    c_sm, st_s = _conv_sample(ga_s, gb_s, conv_state.transpose(1, 0, 2), conv_w, conv_b, g_conv, b_conv)
    c_mix, st_p = _conv_prompt(zt, conv_w, conv_b, g_conv, b_conv,
                               c_sm.transpose(1, 0, 2).reshape(s_batch * s_len, CONV_CH), batch, seq)

    rows_cfg = dict(tm_target=1040, a_buffers=2) if l % 2 == 0 else {}
    down_cfg = dict(tn=2 * MXU_COLS, tk_target=2048) if l % 2 == 0 else dict(tn=4 * MXU_COLS, tk_target=1024)
    y = _matmul_fullk([a_mix, b_mix, c_mix], w_out, l, segments=[(0, D_MODEL)], tn=2 * MXU_COLS, out_dtype=f32,
                      name="proj_out", **rows_cfg)
    xp, xs, h2 = _resid(y, xp, xs, g_post_mix, g_pre_mlp)
    ff = _matmul_fullk([h2], w_up, l, segments=[(0, w_up.shape[2])], tn=2 * MXU_COLS, out_dtype=bf16,
                       relu2=True, name="mlp_up", **rows_cfg)
    y2 = _matmul([ff], w_down, l, n_cols=D_MODEL, out_dtype=f32, name="mlp_down", **down_cfg)
    xp, xs, h_next = _resid(y2, xp, xs, g_post_mlp, g_next)

    kv = lambda a, n: a.reshape(n, WINDOW, N_KV_HEADS, HEAD_DIM)
    prompt_state = (kv(kwin_p, batch), kv(vwin_p, batch), c_p, n_p, m_p[:, :, 0], st_p)
    sample_state = (kv(kwin_s, s_batch), kv(vwin_s, s_batch), n_s, m_s[:, :, 0], st_s.transpose(1, 0, 2))
    return xp, xs, h_next, prompt_state, sample_state, c_s_all


def kernel(x_prompt, x_sample, cache_win_k, cache_win_v, state_mlstm_C, state_mlstm_n, state_mlstm_m, state_conv,
           g_pre_mix, w_in, attn_sinks, b_igate, b_fgate, g_mlstm, conv_w, conv_b, g_conv, b_conv,
           w_out, g_post_mix, g_pre_mlp, w_up, w_down, g_post_mlp):
    batch, seq, d = x_prompt.shape
    s_batch, s_len, _ = x_sample.shape
    depth = w_in.shape[0]
    xp = x_prompt.reshape(batch * seq, d)
    xs = x_sample.reshape(s_batch * s_len, d)
    h = _norm(xp, xs, g_pre_mix[0])
    w_in_t = jnp.swapaxes(w_in, 1, 2)
    p_states, s_states, s_c = [], [], None
    for l in range(depth):
        vectors = (attn_sinks[l], b_igate[l], b_fgate[l], g_mlstm[l], conv_w[l], conv_b[l],
                   g_conv[l], b_conv[l], g_post_mix[l], g_pre_mlp[l], g_post_mlp[l])
        g_next = g_pre_mix[l + 1] if l + 1 < depth else None
        xp, xs, h, ps, ss, s_c = _layer(l, xp, xs, h, batch, seq, s_batch, s_len, cache_win_k[l], cache_win_v[l],
                                        state_mlstm_C, s_c, state_mlstm_n[l], state_mlstm_m[l], state_conv[l],
                                        (w_in_t, w_out, w_up, w_down), vectors, g_next)
        p_states.append(ps)
        s_states.append(ss)
    stack = lambda states, i: jnp.stack([st[i] for st in states])
    s_k, s_v, s_n, s_m, s_cv = (stack(s_states, i) for i in range(5))
    return ((xp.reshape(batch, seq, d), xs.reshape(s_batch, s_len, d))
            + tuple(stack(p_states, i) for i in range(6))
            + (s_k, s_v, s_c, s_n, s_m, s_cv))
```

```python
import functools
import math

import jax
import jax.numpy as jnp
from jax import lax
from jax.experimental import pallas as pl
from jax.experimental.pallas import tpu as pltpu

D_MODEL = 4096
HEAD_DIM = 64
N_Q_HEADS = 16
N_KV_HEADS = 2
Q_PER_KV = N_Q_HEADS // N_KV_HEADS
WINDOW = 128
ROPE_THETA = 10000.0
N_MLSTM_HEADS = 4
MLSTM_DV = 512
MLSTM_DK = 256
CONV_CH = 1024
CONV_WIDTH = 31
NORM_EPS = 1e-6
PAST_LEN = 16384

ATT_WIDTH = N_Q_HEADS * HEAD_DIM
KV_WIDTH = N_KV_HEADS * HEAD_DIM
MLSTM_WIDTH = N_MLSTM_HEADS * MLSTM_DV
COL_QA = 0
COL_KA = COL_QA + ATT_WIDTH
COL_VA = COL_KA + KV_WIDTH
COL_QM = COL_VA + KV_WIDTH
COL_KM = COL_QM + N_MLSTM_HEADS * MLSTM_DK
COL_VM = COL_KM + N_MLSTM_HEADS * MLSTM_DK
COL_GATE = COL_VM + MLSTM_WIDTH
COL_OG = COL_GATE + 2 * N_MLSTM_HEADS
TAIL_COLS = MLSTM_WIDTH + 2 * CONV_CH

LANES = 128
SUBLANES = 8
MXU_COLS = 256
Z1_COLS = 5632
MLSTM_T_PROMPT = 256
MLSTM_T_SAMPLE = 128
CONV_TB = 256
CONV_RSUB = 64
CONV_HIST = 32
VMEM_LIMIT = 56 * 1024 * 1024


def _largest_tile(total, target, multiple):
    best = None
    for t in range(multiple, min(total, target) + 1, multiple):
        if total % t == 0:
            best = t
    assert best is not None, (total, target, multiple)
    return best


def _cparams(sem):
    return pltpu.CompilerParams(dimension_semantics=sem, vmem_limit_bytes=VMEM_LIMIT)


def _rms(x, g):
    return x * lax.rsqrt(jnp.mean(x * x, axis=-1, keepdims=True) + NORM_EPS) * g


def _row_tile(mp, ms):
    return _largest_tile(math.gcd(mp, ms), 256, 16)


def _split_specs(tr, d, nbp):
    prompt = pl.BlockSpec((tr, d), lambda i: (jnp.minimum(i, nbp - 1), 0))
    sample = pl.BlockSpec((tr, d), lambda i: (jnp.maximum(i - nbp, 0), 0))
    return prompt, sample


def _norm_kernel(xp_ref, xs_ref, g_ref, h_ref, *, nbp):
    i = pl.program_id(0)

    @pl.when(i < nbp)
    def _():
        h_ref[...] = _rms(xp_ref[...], g_ref[...]).astype(h_ref.dtype)

    @pl.when(i >= nbp)
    def _():
        h_ref[...] = _rms(xs_ref[...], g_ref[...]).astype(h_ref.dtype)


def _norm(xp, xs, g):
    (mp, d), ms = xp.shape, xs.shape[0]
    tr = _row_tile(mp, ms)
    nbp = mp // tr
    prompt, sample = _split_specs(tr, d, nbp)
    return pl.pallas_call(
        functools.partial(_norm_kernel, nbp=nbp),
        grid=((mp + ms) // tr,),
        in_specs=[prompt, sample, pl.BlockSpec((1, d), lambda i: (0, 0))],
        out_specs=pl.BlockSpec((tr, d), lambda i: (i, 0)),
        out_shape=jax.ShapeDtypeStruct((mp + ms, d), jnp.bfloat16),
        compiler_params=_cparams(("arbitrary",)),
        name="rmsnorm",
    )(xp, xs, g.reshape(1, d))


def _resid_kernel(y_ref, xp_ref, xs_ref, gpost_ref, *rest, nbp, with_next):
    if with_next:
        gnext_ref, xop_ref, xos_ref, h_ref = rest
    else:
        xop_ref, xos_ref = rest
    i = pl.program_id(0)

    def update(x_ref, xo_ref):
        xn = x_ref[...] + _rms(y_ref[...], gpost_ref[...])
        xo_ref[...] = xn
        if with_next:
            h_ref[...] = _rms(xn, gnext_ref[...]).astype(h_ref.dtype)

    pl.when(i < nbp)(lambda: update(xp_ref, xop_ref))
    pl.when(i >= nbp)(lambda: update(xs_ref, xos_ref))


def _resid(y, xp, xs, g_post, g_next):
    (mp, d), ms = xp.shape, xs.shape[0]
    tr = _row_tile(mp, ms)
    nbp = mp // tr
    prompt, sample = _split_specs(tr, d, nbp)
    row = pl.BlockSpec((tr, d), lambda i: (i, 0))
    vec = pl.BlockSpec((1, d), lambda i: (0, 0))
    with_next = g_next is not None
    out_specs = [prompt, sample] + ([row] if with_next else [])
    out_shape = [jax.ShapeDtypeStruct((mp, d), jnp.float32), jax.ShapeDtypeStruct((ms, d), jnp.float32)]
    args = [y, xp, xs, g_post.reshape(1, d)]
    if with_next:
        out_shape.append(jax.ShapeDtypeStruct((mp + ms, d), jnp.bfloat16))
        args.append(g_next.reshape(1, d))
    out = pl.pallas_call(
        functools.partial(_resid_kernel, nbp=nbp, with_next=with_next),
        grid=((mp + ms) // tr,),
        in_specs=[row, prompt, sample, vec] + ([vec] if with_next else []),
        out_specs=out_specs, out_shape=out_shape,
        compiler_params=_cparams(("arbitrary",)),
        name="resid_norm" if with_next else "resid_last",
    )(*args)
    return (out[0], out[1], out[2]) if with_next else (out[0], out[1], None)


def _mm_kernel(*refs, n_a, k_bounds, nk, relu2):
    a_refs, b_ref, o_ref, scratch = refs[:n_a], refs[n_a], refs[n_a + 1], refs[n_a + 2:]
    acc_ref = scratch[0] if scratch else o_ref
    k = pl.program_id(2)

    for t, a_ref in enumerate(a_refs):
        lo, hi = k_bounds[t], k_bounds[t + 1]

        def part(a_ref=a_ref):
            return jnp.dot(a_ref[...], b_ref[...].astype(jnp.bfloat16), preferred_element_type=jnp.float32)

        def assign(part=part):
            acc_ref[...] = part()

        def accumulate(part=part):
            acc_ref[...] += part()

        if lo == 0:
            pl.when(k == 0)(assign)
            if hi > 1:
                pl.when((k > 0) & (k < hi))(accumulate)
        else:
            pl.when((k >= lo) & (k < hi))(accumulate)

    if scratch or relu2:
        @pl.when(k == nk - 1)
        def _():
            r = acc_ref[...]
            if relu2:
                r = jnp.square(jnp.maximum(r, 0.0))
            o_ref[...] = r.astype(o_ref.dtype)


def _matmul(a_list, b, layer, *, n_cols, tn, out_dtype, relu2=False, name):
    m = a_list[0].shape[0]
    tm = _largest_tile(m, 2080, 16)
    tk = _largest_tile(math.gcd(*[a.shape[1] for a in a_list]), 1024, LANES)
    k_bounds = [0]
    for a in a_list:
        k_bounds.append(k_bounds[-1] + a.shape[1] // tk)
    nk = k_bounds[-1]
    assert nk * tk == b.shape[1] and n_cols % tn == 0

    def a_spec(lo, hi):
        return pl.BlockSpec((tm, tk), lambda i, j, k: (i, jnp.clip(k - lo, 0, hi - lo - 1)))

    scratch = [] if out_dtype == jnp.float32 else [pltpu.VMEM((tm, tn), jnp.float32)]
    return pl.pallas_call(
        functools.partial(_mm_kernel, n_a=len(a_list), k_bounds=tuple(k_bounds), nk=nk, relu2=relu2),
        grid=(m // tm, n_cols // tn, nk),
        in_specs=[a_spec(k_bounds[t], k_bounds[t + 1]) for t in range(len(a_list))]
                 + [pl.BlockSpec((None, tk, tn), lambda i, j, k: (layer, k, j))],
        out_specs=pl.BlockSpec((tm, tn), lambda i, j, k: (i, j)),
        out_shape=jax.ShapeDtypeStruct((m, n_cols), out_dtype),
        scratch_shapes=scratch,
        compiler_params=_cparams(("parallel", "parallel", "arbitrary")),
        name=name,
    )(*a_list, b)


def _mm_fullk_kernel(*refs, n_a, relu2, b_transposed, seg_starts):
    a_refs, b_ref, o_refs = refs[:n_a], refs[n_a], refs[n_a + 1:]
    acc, lo = None, 0
    for a_ref in a_refs:
        width = a_ref.shape[1]
        if b_transposed:
            part = lax.dot_general(a_ref[...], b_ref[:, lo:lo + width].astype(jnp.bfloat16),
                                   (((1,), (1,)), ((), ())), preferred_element_type=jnp.float32)
        else:
            part = jnp.dot(a_ref[...], b_ref[lo:lo + width, :].astype(jnp.bfloat16),
                           preferred_element_type=jnp.float32)
        acc = part if acc is None else acc + part
        lo += width
    if relu2:
        acc = jnp.square(jnp.maximum(acc, 0.0))
    if len(o_refs) == 1:
        o_refs[0][...] = acc.astype(o_refs[0].dtype)
    else:
        j = pl.program_id(1)
        for t, o_ref in enumerate(o_refs):
            def store(o_ref=o_ref):
                o_ref[...] = acc.astype(o_ref.dtype)
            pl.when((j >= seg_starts[t]) & (j < seg_starts[t + 1]))(store)


def _matmul_fullk(a_list, b, layer, *, segments, tn, out_dtype, relu2=False, b_transposed=False, name):
    m = a_list[0].shape[0]
    kdim = sum(a.shape[1] for a in a_list)
    tm = _largest_tile(m, 2080, 16)
    seg_starts = [0]
    for col0, n_cols in segments:
        assert n_cols % tn == 0
        seg_starts.append(seg_starts[-1] + n_cols // tn)

    def first_col(j):
        col = segments[0][0] + j * tn
        for t in range(1, len(segments)):
            col = jnp.where(j >= seg_starts[t], segments[t][0] + (j - seg_starts[t]) * tn, col)
        return col

    if b_transposed:
        assert kdim == b.shape[2] and all(c % SUBLANES == 0 for c, _ in segments)
        b_spec = pl.BlockSpec((None, pl.Element(tn), pl.Element(kdim)),
                              lambda i, j: (layer, pl.multiple_of(first_col(j), SUBLANES), 0))
    else:
        assert kdim == b.shape[1] and all(c % tn == 0 for c, _ in segments)
        b_spec = pl.BlockSpec((None, kdim, tn), lambda i, j: (layer, 0, first_col(j) // tn))

    def out_spec(t):
        lo, n = seg_starts[t], seg_starts[t + 1] - seg_starts[t]
        return pl.BlockSpec((tm, tn), lambda i, j: (i, jnp.clip(j - lo, 0, n - 1)))

    outs = pl.pallas_call(
        functools.partial(_mm_fullk_kernel, n_a=len(a_list), relu2=relu2, b_transposed=b_transposed,
                          seg_starts=tuple(seg_starts)),
        grid=(m // tm, seg_starts[-1]),
        in_specs=[pl.BlockSpec((tm, a.shape[1]), lambda i, j: (i, 0), pipeline_mode=pl.Buffered(1))
                  for a in a_list] + [b_spec],
        out_specs=[out_spec(t) for t in range(len(segments))],
        out_shape=[jax.ShapeDtypeStruct((m, n_cols), out_dtype) for _, n_cols in segments],
        compiler_params=_cparams(("parallel", "arbitrary")),
        name=name,
    )(*a_list, b)
    return outs[0] if len(segments) == 1 else outs


def _rope_tables(pos0, length):
    half = HEAD_DIM // 2
    inv_freq = ROPE_THETA ** (-jnp.arange(half, dtype=jnp.float32) / half)
    pos = (pos0 + jnp.arange(length, dtype=jnp.int32)).astype(jnp.float32)
    ang = pos[:, None] * inv_freq[None, :]
    cos, sin = jnp.cos(ang), jnp.sin(ang)
    reps = LANES // HEAD_DIM
    cos_t = jnp.tile(jnp.concatenate([cos, cos], axis=-1), (1, reps))
    sin_t = jnp.tile(jnp.concatenate([-sin, sin], axis=-1), (1, reps))
    return cos_t, sin_t


def _rope(x, cos_t, sin_t):
    half = HEAD_DIM // 2
    lane = lax.broadcasted_iota(jnp.int32, x.shape, 1)
    swapped = jnp.where((lane % HEAD_DIM) < half,
                        pltpu.roll(x, LANES - half, 1), pltpu.roll(x, half, 1))
    return x * cos_t + swapped * sin_t


def _rope_wide(x, cos_t, sin_t):
    return jnp.concatenate(
        [_rope(x[:, c * LANES:(c + 1) * LANES], cos_t, sin_t) for c in range(x.shape[1] // LANES)], axis=-1)


def _attend(q_rot, k_ext, v_ext, mask_fn, sink_ref, stack):
    rq, s_len = q_rot.shape[0], k_ext.shape[0]
    kb = k_ext.astype(jnp.bfloat16)
    vb = v_ext.astype(jnp.bfloat16)
    scale = HEAD_DIM ** -0.5
    rows = lax.broadcasted_iota(jnp.int32, (stack * rq, s_len), 0)
    mask = mask_fn(rows % rq, lax.broadcasted_iota(jnp.int32, (stack * rq, s_len), 1))
    outs = []
    for h0 in range(0, N_Q_HEADS, stack):
        g = h0 // Q_PER_KV
        kg = kb[:, g * HEAD_DIM:(g + 1) * HEAD_DIM]
        vg = vb[:, g * HEAD_DIM:(g + 1) * HEAD_DIM]
        heads = range(h0, h0 + stack)
        qg = jnp.concatenate([q_rot[:, h * HEAD_DIM:(h + 1) * HEAD_DIM] for h in heads], axis=0)
        sink = jnp.concatenate([jnp.full((rq, 1), sink_ref[0, h], jnp.float32) for h in heads], axis=0)
        s = lax.dot_general((qg * scale).astype(jnp.bfloat16), kg, (((1,), (1,)), ((), ())),
                            preferred_element_type=jnp.float32)
        s = jnp.where(mask, s, -jnp.inf)
        m = jnp.maximum(jnp.max(s, axis=-1, keepdims=True), sink)
        p = jnp.exp(s - m)
        denom = jnp.sum(p, axis=-1, keepdims=True) + jnp.exp(sink - m)
        o = jnp.dot(p.astype(jnp.bfloat16), vg, preferred_element_type=jnp.float32) / denom
        outs.extend(o[hh * rq:(hh + 1) * rq] for hh in range(stack))
    return jnp.concatenate(outs, axis=-1)


def _fill_rows(sample_rows, block_rows):
    ms = sample_rows.shape[0]
    padded = -(-ms // block_rows) * block_rows
    return jnp.pad(sample_rows.astype(jnp.bfloat16), ((0, padded - ms), (0, 0)))


def _attn_prompt_kernel(sink_ref, q_ref, kc_ref, kp_ref, vc_ref, vp_ref,
                        cosc_ref, sinc_ref, cosp_ref, sinp_ref, fill_ref,
                        o_ref, kwin_ref, vwin_ref, *, nb, nbp):
    i = pl.program_id(0)
    j = i % nb

    @pl.when(i < nbp)
    def _():
        cosc, sinc = cosc_ref[...], sinc_ref[...]
        q_rot = _rope_wide(q_ref[...], cosc, sinc)
        k_cur = _rope(kc_ref[...], cosc, sinc)
        k_prev = _rope(kp_ref[...], cosp_ref[...], sinp_ref[...])
        k_ext = jnp.concatenate([k_prev, k_cur], axis=0)
        v_ext = jnp.concatenate([vp_ref[...], vc_ref[...]], axis=0)

        def mask_fn(tq, sk):
            return ((sk >= tq) & (sk < WINDOW) & (j > 0)) | ((sk >= WINDOW) & ((sk - WINDOW) <= tq))

        o_ref[...] = _attend(q_rot, k_ext, v_ext, mask_fn, sink_ref, 1).astype(o_ref.dtype)

        @pl.when(j == nb - 1)
        def _():
            kwin_ref[0] = k_cur
            vwin_ref[0] = vc_ref[...]

    @pl.when(i >= nbp)
    def _():
        o_ref[...] = fill_ref[...]


def _attn_prompt(z1, sinks, a_sample, batch, seq):
    nb = seq // WINDOW
    nbp = batch * nb
    fill = _fill_rows(a_sample, WINDOW)
    nbs = fill.shape[0] // WINDOW
    cos_t, sin_t = _rope_tables(0, seq)
    cur = lambda i: jnp.minimum(i, nbp - 1)
    prev = lambda i: cur(i) - (cur(i) % nb > 0)
    blk = lambda col: pl.BlockSpec((WINDOW, LANES), lambda i: (cur(i), col))
    blk_prev = lambda col: pl.BlockSpec((WINDOW, LANES), lambda i: (prev(i), col))
    tab = pl.BlockSpec((WINDOW, LANES), lambda i: (cur(i) % nb, 0))
    tab_prev = pl.BlockSpec((WINDOW, LANES), lambda i: (prev(i) % nb, 0))
    win = pl.BlockSpec((1, WINDOW, KV_WIDTH), lambda i: (cur(i) // nb, 0, 0))
    return pl.pallas_call(
        functools.partial(_attn_prompt_kernel, nb=nb, nbp=nbp),
        grid=(nbp + nbs,),
        in_specs=[pl.BlockSpec(memory_space=pltpu.SMEM),
                  pl.BlockSpec((WINDOW, ATT_WIDTH), lambda i: (cur(i), 0)),
                  blk(COL_KA // LANES), blk_prev(COL_KA // LANES),
                  blk(COL_VA // LANES), blk_prev(COL_VA // LANES),
                  tab, tab, tab_prev, tab_prev,
                  pl.BlockSpec((WINDOW, ATT_WIDTH), lambda i: (jnp.maximum(i - nbp, 0), 0))],
        out_specs=[pl.BlockSpec((WINDOW, ATT_WIDTH), lambda i: (i, 0)), win, win],
        out_shape=[jax.ShapeDtypeStruct((z1.shape[0], ATT_WIDTH), jnp.bfloat16),
                   jax.ShapeDtypeStruct((batch, WINDOW, KV_WIDTH), jnp.float32),
                   jax.ShapeDtypeStruct((batch, WINDOW, KV_WIDTH), jnp.float32)],
        compiler_params=_cparams(("arbitrary",)),
        name="attn_prompt",
    )(sinks.reshape(1, N_Q_HEADS), z1, z1, z1, z1, z1, cos_t, sin_t, cos_t, sin_t, fill)


def _attn_sample_kernel(sink_ref, z_ref, ck_ref, cv_ref, cos_ref, sin_ref,
                        o_ref, kout_ref, vout_ref, kext_ref, vext_ref, *, n_new, rows):
    cos_t, sin_t = cos_ref[...], sin_ref[...]
    z = z_ref[0]
    q_rot = _rope_wide(z[:, COL_QA:COL_QA + ATT_WIDTH], cos_t, sin_t)
    k_new = _rope(z[:, COL_KA:COL_KA + KV_WIDTH], cos_t, sin_t)
    v_new = z[:, COL_VA:COL_VA + KV_WIDTH]
    kext_ref[0:WINDOW, :] = ck_ref[0]
    vext_ref[0:WINDOW, :] = cv_ref[0]
    kext_ref[WINDOW:WINDOW + rows, :] = k_new
    vext_ref[WINDOW:WINDOW + rows, :] = v_new
    fill = jnp.zeros((WINDOW - rows, KV_WIDTH), jnp.float32)
    kext_ref[WINDOW + rows:, :] = fill
    vext_ref[WINDOW + rows:, :] = fill

    def mask_fn(tq, sk):
        return ((sk >= tq) & (sk < WINDOW)) | ((sk >= WINDOW) & ((sk - WINDOW) <= tq) & (sk < WINDOW + n_new))

    o_ref[0] = _attend(q_rot, kext_ref[...], vext_ref[...], mask_fn, sink_ref, Q_PER_KV)
    kout_ref[0] = kext_ref[pl.ds(n_new, WINDOW), :]
    vout_ref[0] = vext_ref[pl.ds(n_new, WINDOW), :]


def _attn_sample(zs, cache_k, cache_v, sinks, n_new):
    s, rows, _ = zs.shape
    cos_t, sin_t = _rope_tables(PAST_LEN, rows)
    seq_blk = lambda r, c: pl.BlockSpec((1, r, c), lambda n: (n, 0, 0))
    tab = pl.BlockSpec((rows, LANES), lambda n: (0, 0))
    return pl.pallas_call(
        functools.partial(_attn_sample_kernel, n_new=n_new, rows=rows),
        grid=(s,),
        in_specs=[pl.BlockSpec(memory_space=pltpu.SMEM),
                  seq_blk(rows, Z1_COLS), seq_blk(WINDOW, KV_WIDTH), seq_blk(WINDOW, KV_WIDTH), tab, tab],
        out_specs=[seq_blk(rows, ATT_WIDTH), seq_blk(WINDOW, KV_WIDTH), seq_blk(WINDOW, KV_WIDTH)],
        out_shape=[jax.ShapeDtypeStruct((s, rows, ATT_WIDTH), jnp.float32),
                   jax.ShapeDtypeStruct((s, WINDOW, KV_WIDTH), jnp.float32),
                   jax.ShapeDtypeStruct((s, WINDOW, KV_WIDTH), jnp.float32)],
        scratch_shapes=[pltpu.VMEM((2 * WINDOW, KV_WIDTH), jnp.float32),
                        pltpu.VMEM((2 * WINDOW, KV_WIDTH), jnp.float32)],
        compiler_params=_cparams(("parallel",)),
        name="attn_sample",
    )(sinks.reshape(1, N_Q_HEADS), zs, cache_k, cache_v, cos_t, sin_t)


def _log_sigmoid(x):
    return jnp.minimum(x, 0.0) - jnp.log(1.0 + jnp.exp(-jnp.abs(x)))


def _mlstm_chunk(q, k, v, li_col, lf_col, li_row, lf_row, c_state, n_state, m_state):
    t = q.shape[0]
    r = lax.broadcasted_iota(jnp.int32, (t, t), 0)
    c = lax.broadcasted_iota(jnp.int32, (t, t), 1)
    tril = c <= r
    b_col = jnp.sum(jnp.where(tril, lf_row, 0.0), axis=1, keepdims=True)
    b_row = jnp.sum(jnp.where(r <= c, lf_col, 0.0), axis=0, keepdims=True)
    b_last = jnp.sum(lf_row, axis=1, keepdims=True)
    log_d = jnp.where(tril, b_col - b_row + li_row, -jnp.inf)
    inter = b_col + m_state
    m_row = jnp.maximum(inter, jnp.max(log_d, axis=1, keepdims=True))
    qb = q.astype(jnp.bfloat16)
    kb = k.astype(jnp.bfloat16)
    vb = v.astype(jnp.bfloat16)
    qk = lax.dot_general(qb, kb, (((1,), (1,)), ((), ())), preferred_element_type=jnp.float32)
    s = qk * jnp.exp(log_d - m_row)
    w_inter = jnp.exp(inter - m_row)
    cq = lax.dot_general(qb, c_state.astype(jnp.bfloat16), (((1,), (1,)), ((), ())),
                         preferred_element_type=jnp.float32)
    num = jnp.dot(s.astype(jnp.bfloat16), vb, preferred_element_type=jnp.float32) + w_inter * cq
    den = jnp.sum(s, axis=1, keepdims=True) + w_inter * jnp.sum(q * n_state, axis=1, keepdims=True)
    h = num / jnp.maximum(jnp.abs(den), jnp.exp(-m_row))
    g_row = b_last - b_row + li_row
    g_col = b_last - b_col + li_col
    m_new = jnp.maximum(b_last + m_state, jnp.max(g_row, axis=1, keepdims=True))
    decay = jnp.exp(b_last + m_state - m_new)
    wk_col = jnp.exp(g_col - m_new)
    wv_t = (wk_col * v).T.astype(jnp.bfloat16)
    c_new = decay * c_state + jnp.dot(wv_t, kb, preferred_element_type=jnp.float32)
    n_new = decay * n_state + jnp.sum(wk_col * k, axis=0, keepdims=True)
    return h, c_new, n_new, m_new


def _mlstm_heads(bias_ref, z, og, zgt, gm_ref, c_s, n_s, m_s, write_out, *, t, n_valid):
    rows = z.shape[0]
    zg = z[:, COL_GATE:COL_GATE + LANES]
    if rows < t:
        pad = lambda a: jnp.concatenate([a, jnp.zeros((t - rows, a.shape[1]), a.dtype)], axis=0)
        z, og, zg = pad(z), pad(og), pad(zg)
    valid_col = lax.broadcasted_iota(jnp.int32, (t, 1), 0) < n_valid
    valid_row = lax.broadcasted_iota(jnp.int32, (1, t), 1) < n_valid
    for h in range(N_MLSTM_HEADS):
        q = z[:, COL_QM + h * MLSTM_DK:COL_QM + (h + 1) * MLSTM_DK]
        k = z[:, COL_KM + h * MLSTM_DK:COL_KM + (h + 1) * MLSTM_DK] * (MLSTM_DK ** -0.5)
        v = z[:, COL_VM + h * MLSTM_DV:COL_VM + (h + 1) * MLSTM_DV]
        b_i = bias_ref[0, h]
        b_f = bias_ref[1, h]
        li_col = zg[:, h:h + 1] + b_i
        lf_col = _log_sigmoid(zg[:, N_MLSTM_HEADS + h:N_MLSTM_HEADS + h + 1] + b_f)
        li_row = zgt[h:h + 1, :] + b_i
        lf_row = _log_sigmoid(zgt[N_MLSTM_HEADS + h:N_MLSTM_HEADS + h + 1, :] + b_f)
        if n_valid < t:
            li_col = jnp.where(valid_col, li_col, -jnp.inf)
            lf_col = jnp.where(valid_col, lf_col, 0.0)
            li_row = jnp.where(valid_row, li_row, -jnp.inf)
            lf_row = jnp.where(valid_row, lf_row, 0.0)
        hm, c_new, n_new, m_new = _mlstm_chunk(
            q, k, v, li_col, lf_col, li_row, lf_row, c_s[h], n_s[h:h + 1, :], m_s[h:h + 1, 0:1])
        c_s[h] = c_new
        n_s[h:h + 1, :] = n_new
        m_s[h:h + 1, :] = jnp.broadcast_to(m_new, (1, LANES))
        gate = jax.nn.sigmoid(og[:, h * MLSTM_DV:(h + 1) * MLSTM_DV])
        write_out(h, (_rms(hm, gm_ref[h:h + 1, :]) * gate)[:rows])


def _mlstm_prompt_kernel(bias_ref, z_ref, og_ref, zgt_ref, gm_ref, fill_ref,
                         o_ref, co_ref, no_ref, mo_ref, c_s, n_s, m_s, *, t, nc, nbp):
    i = pl.program_id(0)
    step = i % nc

    @pl.when(i < nbp)
    def _():
        @pl.when(step == 0)
        def _():
            c_s[...] = jnp.zeros_like(c_s)
            n_s[...] = jnp.zeros_like(n_s)
            m_s[...] = jnp.zeros_like(m_s)

        def write_out(h, out):
            o_ref[:, h * MLSTM_DV:(h + 1) * MLSTM_DV] = out.astype(o_ref.dtype)

        _mlstm_heads(bias_ref, z_ref[...], og_ref[...], zgt_ref[...], gm_ref, c_s, n_s, m_s, write_out,
                     t=t, n_valid=t)

        @pl.when(step == nc - 1)
        def _():
            co_ref[0] = c_s[...]
            no_ref[0] = n_s[...]
            mo_ref[0] = m_s[...]

    @pl.when(i >= nbp)
    def _():
        o_ref[...] = fill_ref[...]


def _mlstm_sample_kernel(bias_ref, z_ref, og_ref, zgt_ref, gm_ref, c0_ref, n0_ref, m0_ref, *rest,
                         t, n_valid, layer, first_call):
    o_ref, co_ref, no_ref, mo_ref, c_s, n_s, m_s = rest if first_call else rest[1:]
    c_s[...] = c0_ref[0]
    n_s[...] = n0_ref[0]
    m_s[...] = m0_ref[0]

    def write_out(h, out):
        o_ref[0, :, h * MLSTM_DV:(h + 1) * MLSTM_DV] = out

    _mlstm_heads(bias_ref, z_ref[0], og_ref[0], zgt_ref[0], gm_ref, c_s, n_s, m_s, write_out,
                 t=t, n_valid=n_valid)
    if first_call:
        for other in range(co_ref.shape[0]):
            co_ref[other, 0] = c_s[...] if other == layer else jnp.zeros_like(c_s)
    else:
        co_ref[0] = c_s[...]
    no_ref[0] = n_s[...]
    mo_ref[0] = m_s[...]


_MLSTM_SCRATCH = [pltpu.VMEM((N_MLSTM_HEADS, MLSTM_DV, MLSTM_DK), jnp.float32),
                  pltpu.VMEM((N_MLSTM_HEADS, MLSTM_DK), jnp.float32),
                  pltpu.VMEM((N_MLSTM_HEADS, LANES), jnp.float32)]


def _mlstm_state_shapes(n):
    return [jax.ShapeDtypeStruct((n, N_MLSTM_HEADS, MLSTM_DV, MLSTM_DK), jnp.float32),
            jax.ShapeDtypeStruct((n, N_MLSTM_HEADS, MLSTM_DK), jnp.float32),
            jax.ShapeDtypeStruct((n, N_MLSTM_HEADS, LANES), jnp.float32)]


def _mlstm_state_specs(index):
    return [pl.BlockSpec((1, N_MLSTM_HEADS, MLSTM_DV, MLSTM_DK), lambda i: (index(i), 0, 0, 0)),
            pl.BlockSpec((1, N_MLSTM_HEADS, MLSTM_DK), lambda i: (index(i), 0, 0)),
            pl.BlockSpec((1, N_MLSTM_HEADS, LANES), lambda i: (index(i), 0, 0))]


def _mlstm_prompt(z1, ztail, zgt, bias, g_mlstm, b_sample, batch, seq):
    t = MLSTM_T_PROMPT
    nc = seq // t
    nbp = batch * nc
    fill = _fill_rows(b_sample, t)
    nbs = fill.shape[0] // t
    cur = lambda i: jnp.minimum(i, nbp - 1)
    row = lambda w: pl.BlockSpec((t, w), lambda i: (cur(i), 0))
    return pl.pallas_call(
        functools.partial(_mlstm_prompt_kernel, t=t, nc=nc, nbp=nbp),
        grid=(nbp + nbs,),
        in_specs=[pl.BlockSpec(memory_space=pltpu.SMEM),
                  row(Z1_COLS), row(MLSTM_WIDTH),
                  pl.BlockSpec((SUBLANES, t), lambda i: (0, cur(i))),
                  pl.BlockSpec((N_MLSTM_HEADS, MLSTM_DV), lambda i: (0, 0)),
                  pl.BlockSpec((t, MLSTM_WIDTH), lambda i: (jnp.maximum(i - nbp, 0), 0))],
        out_specs=[pl.BlockSpec((t, MLSTM_WIDTH), lambda i: (i, 0))] + _mlstm_state_specs(lambda i: cur(i) // nc),
        out_shape=[jax.ShapeDtypeStruct((z1.shape[0], MLSTM_WIDTH), jnp.bfloat16)] + _mlstm_state_shapes(batch),
        scratch_shapes=_MLSTM_SCRATCH,
        compiler_params=_cparams(("arbitrary",)),
        name="mlstm_prompt",
    )(bias, z1, ztail, zgt, g_mlstm, fill)


def _mlstm_sample(zs, zts, zgts, bias, g_mlstm, c0_all, n0, m0, n_valid, layer, c_out_prev):
    s, rows, _ = zs.shape
    t = MLSTM_T_SAMPLE
    depth = c0_all.shape[0]
    first_call = c_out_prev is None
    seq_blk = lambda r, c: pl.BlockSpec((1, r, c), lambda n: (n, 0, 0))
    c_in = pl.BlockSpec((None, 1, N_MLSTM_HEADS, MLSTM_DV, MLSTM_DK), lambda n: (layer, n, 0, 0, 0))
    c_out = (pl.BlockSpec((depth, 1, N_MLSTM_HEADS, MLSTM_DV, MLSTM_DK), lambda n: (0, n, 0, 0, 0))
             if first_call else c_in)
    small = _mlstm_state_specs(lambda n: n)[1:]
    in_specs = [pl.BlockSpec(memory_space=pltpu.SMEM),
                seq_blk(rows, Z1_COLS), seq_blk(rows, MLSTM_WIDTH), seq_blk(SUBLANES, t),
                pl.BlockSpec((N_MLSTM_HEADS, MLSTM_DV), lambda n: (0, 0)), c_in] + small
    args = [bias, zs, zts, zgts, g_mlstm, c0_all, n0, m0]
    aliases = {}
    if not first_call:
        in_specs.append(pl.BlockSpec(memory_space=pl.ANY))
        args.append(c_out_prev)
        aliases = {len(args) - 1: 1}
    return pl.pallas_call(
        functools.partial(_mlstm_sample_kernel, t=t, n_valid=n_valid, layer=layer, first_call=first_call),
        grid=(s,),
        in_specs=in_specs,
        out_specs=[seq_blk(rows, MLSTM_WIDTH), c_out] + small,
        out_shape=[jax.ShapeDtypeStruct((s, rows, MLSTM_WIDTH), jnp.float32),
                   jax.ShapeDtypeStruct((depth,) + c0_all.shape[1:], jnp.float32)] + _mlstm_state_shapes(s)[1:],
        input_output_aliases=aliases,
        scratch_shapes=_MLSTM_SCRATCH,
        compiler_params=_cparams(("arbitrary",)),
        name="mlstm_sample",
    )(*args)


def _conv_norm_act(cv, cb, g, b):
    cv = cv + cb
    mu = jnp.mean(cv, axis=-1, keepdims=True)
    d = cv - mu
    var = jnp.mean(d * d, axis=-1, keepdims=True)
    y = d * lax.rsqrt(var + NORM_EPS) * g + b
    return y * jax.nn.sigmoid(y)


def _conv_prompt_kernel(ga_ref, gb_ref, w_ref, cb_ref, g_ref, b_ref, fill_ref, o_ref, st_ref, ext_ref, cv_ref,
                        *, nt, nbp):
    tb, rsub = CONV_TB, CONV_RSUB
    i = pl.program_id(0)
    step = i % nt

    @pl.when(i < nbp)
    def _():
        @pl.when(step == 0)
        def _():
            ext_ref[0:CONV_HIST, :] = jnp.zeros((CONV_HIST, CONV_CH), jnp.float32)

        ext_ref[CONV_HIST:, :] = ga_ref[...] * jax.nn.sigmoid(gb_ref[...])
        off = CONV_HIST - (CONV_WIDTH - 1)
        span = rsub + CONV_HIST
        for lc in range(CONV_CH // LANES):
            cols = slice(lc * LANES, (lc + 1) * LANES)
            for r0 in range(0, tb, rsub):
                e = ext_ref[r0:r0 + span, cols]
                acc = None
                for s in range(SUBLANES):
                    zs = e if s == 0 else pltpu.roll(e, span - s, 0)
                    for a in range(CONV_HIST // SUBLANES + 1):
                        j = SUBLANES * a + s - off
                        if 0 <= j < CONV_WIDTH:
                            term = w_ref[j:j + 1, cols] * zs[SUBLANES * a:SUBLANES * a + rsub]
                            acc = term if acc is None else acc + term
                cv_ref[r0:r0 + rsub, cols] = acc
        o_ref[...] = _conv_norm_act(cv_ref[...], cb_ref[...], g_ref[...], b_ref[...]).astype(o_ref.dtype)

        @pl.when(step == nt - 1)
        def _():
            st_ref[0] = ext_ref[tb + off:tb + CONV_HIST, :]

        ext_ref[0:CONV_HIST, :] = ext_ref[tb:tb + CONV_HIST, :]

    @pl.when(i >= nbp)
    def _():
        o_ref[...] = fill_ref[...]


def _conv_prompt(ztail, conv_w, conv_b, g_conv, b_conv, c_sample, batch, seq):
    tb = CONV_TB
    nt = seq // tb
    nbp = batch * nt
    fill = _fill_rows(c_sample, tb)
    nbs = fill.shape[0] // tb
    cur = lambda i: jnp.minimum(i, nbp - 1)
    vec = pl.BlockSpec((1, CONV_CH), lambda i: (0, 0))
    row = lambda col: pl.BlockSpec((tb, CONV_CH), lambda i: (cur(i), col))
    return pl.pallas_call(
        functools.partial(_conv_prompt_kernel, nt=nt, nbp=nbp),
        grid=(nbp + nbs,),
        in_specs=[row(MLSTM_WIDTH // CONV_CH), row(MLSTM_WIDTH // CONV_CH + 1),
                  pl.BlockSpec((CONV_WIDTH, CONV_CH), lambda i: (0, 0)), vec, vec, vec,
                  pl.BlockSpec((tb, CONV_CH), lambda i: (jnp.maximum(i - nbp, 0), 0))],
        out_specs=[pl.BlockSpec((tb, CONV_CH), lambda i: (i, 0)),
                   pl.BlockSpec((1, CONV_WIDTH - 1, CONV_CH), lambda i: (cur(i) // nt, 0, 0))],
        out_shape=[jax.ShapeDtypeStruct((ztail.shape[0], CONV_CH), jnp.bfloat16),
                   jax.ShapeDtypeStruct((batch, CONV_WIDTH - 1, CONV_CH), jnp.float32)],
        scratch_shapes=[pltpu.VMEM((tb + CONV_HIST, CONV_CH), jnp.float32),
                        pltpu.VMEM((tb, CONV_CH), jnp.float32)],
        compiler_params=_cparams(("arbitrary",)),
        name="conv_prompt",
    )(ztail, ztail, conv_w, conv_b.reshape(1, -1), g_conv.reshape(1, -1), b_conv.reshape(1, -1), fill)


def _conv_sample_kernel(ga_ref, gb_ref, st_ref, w_ref, cb_ref, g_ref, b_ref, o_ref, sto_ref, ext_ref, *, n_new):
    hist = CONV_WIDTH - 1
    ext_ref[0:hist] = st_ref[...]
    ext_ref[hist:] = ga_ref[...] * jax.nn.sigmoid(gb_ref[...])
    for t in range(n_new):
        acc = w_ref[0:1, :] * ext_ref[t]
        for j in range(1, CONV_WIDTH):
            acc = acc + w_ref[j:j + 1, :] * ext_ref[t + j]
        o_ref[t] = _conv_norm_act(acc, cb_ref[...], g_ref[...], b_ref[...])
    sto_ref[...] = ext_ref[n_new:]


def _conv_sample(ga, gb, state, conv_w, conv_b, g_conv, b_conv):
    n_new, s, ch = ga.shape
    hist = CONV_WIDTH - 1
    full = lambda shape: pl.BlockSpec(shape, lambda i: (0,) * len(shape))
    return pl.pallas_call(
        functools.partial(_conv_sample_kernel, n_new=n_new),
        grid=(1,),
        in_specs=[full((n_new, s, ch)), full((n_new, s, ch)), full((hist, s, ch)),
                  full((CONV_WIDTH, ch)), full((1, ch)), full((1, ch)), full((1, ch))],
        out_specs=[full((n_new, s, ch)), full((hist, s, ch))],
        out_shape=[jax.ShapeDtypeStruct((n_new, s, ch), jnp.float32),
                   jax.ShapeDtypeStruct((hist, s, ch), jnp.float32)],
        scratch_shapes=[pltpu.VMEM((hist + n_new, s, ch), jnp.float32)],
        compiler_params=_cparams(("arbitrary",)),
        name="conv_sample",
    )(ga, gb, state, conv_w, conv_b.reshape(1, -1), g_conv.reshape(1, -1), b_conv.reshape(1, -1))


def _layer(l, xp, xs, h, batch, seq, s_batch, s_len, ck, cv, c0_all, c_out_prev, n0, m0, conv_state,
           weights, vectors, g_next):
    w_in_t, w_out, w_up, w_down = weights
    (sinks, b_ig, b_fg, g_mlstm, conv_w, conv_b, g_conv, b_conv, g_post_mix, g_pre_mlp, g_post_mlp) = vectors
    mp = batch * seq
    rows = SUBLANES
    assert s_len <= rows
    f32, bf16 = jnp.float32, jnp.bfloat16

    z1, zt = _matmul_fullk([h], w_in_t, l, segments=[(0, Z1_COLS), (COL_OG, TAIL_COLS)], tn=2 * MXU_COLS,
                           out_dtype=f32, b_transposed=True, name="proj_in")
    gates = z1[:, COL_GATE:COL_OG]
    zgt = gates.T
    bias = jnp.stack([b_ig, b_fg])

    def pad_rows(a):
        a = a.reshape(s_batch, s_len, a.shape[-1])
        return jnp.pad(a, ((0, 0), (0, rows - s_len), (0, 0)))

    zs, zts = pad_rows(z1[mp:]), pad_rows(zt[mp:])
    zgts = jnp.pad(gates[mp:].reshape(s_batch, s_len, 2 * N_MLSTM_HEADS).transpose(0, 2, 1),
                   ((0, 0), (0, 0), (0, MLSTM_T_SAMPLE - s_len)))
    unpad = lambda a: a[:, :s_len].reshape(s_batch * s_len, a.shape[-1])

    a_s, kwin_s, vwin_s = _attn_sample(zs, ck.reshape(s_batch, WINDOW, KV_WIDTH),
                                       cv.reshape(s_batch, WINDOW, KV_WIDTH), sinks, s_len)
    a_mix, kwin_p, vwin_p = _attn_prompt(z1, sinks, unpad(a_s), batch, seq)
    m0b = jnp.broadcast_to(m0[:, :, None], m0.shape + (LANES,))
    b_s, c_s_all, n_s, m_s = _mlstm_sample(zs, zts, zgts, bias, g_mlstm, c0_all, n0, m0b, s_len, l, c_out_prev)
    b_mix, c_p, n_p, m_p = _mlstm_prompt(z1, zt, zgt, bias, g_mlstm, unpad(b_s), batch, seq)
    tm = lambda a: a.reshape(s_batch, s_len, CONV_CH).transpose(1, 0, 2)
    ga_s = tm(zt[mp:, MLSTM_WIDTH:MLSTM_WIDTH + CONV_CH])
    gb_s = tm(zt[mp:, MLSTM_WIDTH + CONV_CH:])
    c_sm, st_s = _conv_sample(ga_s, gb_s, conv_state.transpose(1, 0, 2), conv_w, conv_b, g_conv, b_conv)
    c_mix, st_p = _conv_prompt(zt, conv_w, conv_b, g_conv, b_conv,
                               c_sm.transpose(1, 0, 2).reshape(s_batch * s_len, CONV_CH), batch, seq)

    y = _matmul_fullk([a_mix, b_mix, c_mix], w_out, l, segments=[(0, D_MODEL)], tn=2 * MXU_COLS, out_dtype=f32,
                      name="proj_out")
    xp, xs, h2 = _resid(y, xp, xs, g_post_mix, g_pre_mlp)
    ff = _matmul_fullk([h2], w_up, l, segments=[(0, w_up.shape[2])], tn=2 * MXU_COLS, out_dtype=bf16,
                       relu2=True, name="mlp_up")
    y2 = _matmul([ff], w_down, l, n_cols=D_MODEL, tn=4 * MXU_COLS, out_dtype=f32, name="mlp_down")
    xp, xs, h_next = _resid(y2, xp, xs, g_post_mlp, g_next)

    kv = lambda a, n: a.reshape(n, WINDOW, N_KV_HEADS, HEAD_DIM)
    prompt_state = (kv(kwin_p, batch), kv(vwin_p, batch), c_p, n_p, m_p[:, :, 0], st_p)
    sample_state = (kv(kwin_s, s_batch), kv(vwin_s, s_batch), n_s, m_s[:, :, 0], st_s.transpose(1, 0, 2))
    return xp, xs, h_next, prompt_state, sample_state, c_s_all


def kernel(x_prompt, x_sample, cache_win_k, cache_win_v, state_mlstm_C, state_mlstm_n, state_mlstm_m, state_conv,
           g_pre_mix, w_in, attn_sinks, b_igate, b_fgate, g_mlstm, conv_w, conv_b, g_conv, b_conv,
           w_out, g_post_mix, g_pre_mlp, w_up, w_down, g_post_mlp):
    batch, seq, d = x_prompt.shape
    s_batch, s_len, _ = x_sample.shape
    depth = w_in.shape[0]
    xp = x_prompt.reshape(batch * seq, d)
    xs = x_sample.reshape(s_batch * s_len, d)
    h = _norm(xp, xs, g_pre_mix[0])
    w_in_t = jnp.swapaxes(w_in, 1, 2)
    p_states, s_states, s_c = [], [], None
    for l in range(depth):
        vectors = (attn_sinks[l], b_igate[l], b_fgate[l], g_mlstm[l], conv_w[l], conv_b[l],
                   g_conv[l], b_conv[l], g_post_mix[l], g_pre_mlp[l], g_post_mlp[l])
        g_next = g_pre_mix[l + 1] if l + 1 < depth else None
        xp, xs, h, ps, ss, s_c = _layer(l, xp, xs, h, batch, seq, s_batch, s_len, cache_win_k[l], cache_win_v[l],
                                        state_mlstm_C, s_c, state_mlstm_n[l], state_mlstm_m[l], state_conv[l],
                                        (w_in_t, w_out, w_up, w_down), vectors, g_next)
        p_states.append(ps)
        s_states.append(ss)
    stack = lambda states, i: jnp.stack([st[i] for st in states])
    s_k, s_v, s_n, s_m, s_cv = (stack(s_states, i) for i in range(5))
    return ((xp.reshape(batch, seq, d), xs.reshape(s_batch, s_len, d))
            + tuple(stack(p_states, i) for i in range(6))
            + (s_k, s_v, s_c, s_n, s_m, s_cv))
```

```python
import functools
import math

import jax
import jax.numpy as jnp
from jax import lax
from jax.experimental import pallas as pl
from jax.experimental.pallas import tpu as pltpu

D_MODEL = 4096
HEAD_DIM = 64
N_Q_HEADS = 16
N_KV_HEADS = 2
Q_PER_KV = N_Q_HEADS // N_KV_HEADS
WINDOW = 128
ROPE_THETA = 10000.0
N_MLSTM_HEADS = 4
MLSTM_DV = 512
MLSTM_DK = 256
CONV_CH = 1024
CONV_WIDTH = 31
NORM_EPS = 1e-6
PAST_LEN = 16384

ATT_WIDTH = N_Q_HEADS * HEAD_DIM
KV_WIDTH = N_KV_HEADS * HEAD_DIM
MLSTM_WIDTH = N_MLSTM_HEADS * MLSTM_DV
COL_QA = 0
COL_KA = COL_QA + ATT_WIDTH
COL_VA = COL_KA + KV_WIDTH
COL_QM = COL_VA + KV_WIDTH
COL_KM = COL_QM + N_MLSTM_HEADS * MLSTM_DK
COL_VM = COL_KM + N_MLSTM_HEADS * MLSTM_DK
COL_GATE = COL_VM + MLSTM_WIDTH
COL_OG = COL_GATE + 2 * N_MLSTM_HEADS
TAIL_COLS = MLSTM_WIDTH + 2 * CONV_CH

LANES = 128
SUBLANES = 8
MXU_COLS = 256
Z1_COLS = 5632
MLSTM_T_PROMPT = 512
MLSTM_T_SAMPLE = 128
CONV_TB = 256
CONV_RSUB = 64
CONV_HIST = 32
VMEM_LIMIT = 56 * 1024 * 1024


def _largest_tile(total, target, multiple):
    best = None
    for t in range(multiple, min(total, target) + 1, multiple):
        if total % t == 0:
            best = t
    assert best is not None, (total, target, multiple)
    return best


def _cparams(sem):
    return pltpu.CompilerParams(dimension_semantics=sem, vmem_limit_bytes=VMEM_LIMIT)


def _rms(x, g):
    return x * lax.rsqrt(jnp.mean(x * x, axis=-1, keepdims=True) + NORM_EPS) * g


def _row_tile(mp, ms):
    return _largest_tile(math.gcd(mp, ms), 256, 16)


def _split_specs(tr, d, nbp):
    prompt = pl.BlockSpec((tr, d), lambda i: (jnp.minimum(i, nbp - 1), 0))
    sample = pl.BlockSpec((tr, d), lambda i: (jnp.maximum(i - nbp, 0), 0))
    return prompt, sample


def _norm_kernel(xp_ref, xs_ref, g_ref, h_ref, *, nbp):
    i = pl.program_id(0)

    @pl.when(i < nbp)
    def _():
        h_ref[...] = _rms(xp_ref[...], g_ref[...]).astype(h_ref.dtype)

    @pl.when(i >= nbp)
    def _():
        h_ref[...] = _rms(xs_ref[...], g_ref[...]).astype(h_ref.dtype)


def _norm(xp, xs, g):
    (mp, d), ms = xp.shape, xs.shape[0]
    tr = _row_tile(mp, ms)
    nbp = mp // tr
    prompt, sample = _split_specs(tr, d, nbp)
    return pl.pallas_call(
        functools.partial(_norm_kernel, nbp=nbp),
        grid=((mp + ms) // tr,),
        in_specs=[prompt, sample, pl.BlockSpec((1, d), lambda i: (0, 0))],
        out_specs=pl.BlockSpec((tr, d), lambda i: (i, 0)),
        out_shape=jax.ShapeDtypeStruct((mp + ms, d), jnp.bfloat16),
        compiler_params=_cparams(("arbitrary",)),
        name="rmsnorm",
    )(xp, xs, g.reshape(1, d))


def _resid_kernel(y_ref, xp_ref, xs_ref, gpost_ref, *rest, nbp, with_next):
    if with_next:
        gnext_ref, xop_ref, xos_ref, h_ref = rest
    else:
        xop_ref, xos_ref = rest
    i = pl.program_id(0)

    def update(x_ref, xo_ref):
        xn = x_ref[...] + _rms(y_ref[...], gpost_ref[...])
        xo_ref[...] = xn
        if with_next:
            h_ref[...] = _rms(xn, gnext_ref[...]).astype(h_ref.dtype)

    pl.when(i < nbp)(lambda: update(xp_ref, xop_ref))
    pl.when(i >= nbp)(lambda: update(xs_ref, xos_ref))


def _resid(y, xp, xs, g_post, g_next):
    (mp, d), ms = xp.shape, xs.shape[0]
    tr = _row_tile(mp, ms)
    nbp = mp // tr
    prompt, sample = _split_specs(tr, d, nbp)
    row = pl.BlockSpec((tr, d), lambda i: (i, 0))
    vec = pl.BlockSpec((1, d), lambda i: (0, 0))
    with_next = g_next is not None
    out_specs = [prompt, sample] + ([row] if with_next else [])
    out_shape = [jax.ShapeDtypeStruct((mp, d), jnp.float32), jax.ShapeDtypeStruct((ms, d), jnp.float32)]
    args = [y, xp, xs, g_post.reshape(1, d)]
    if with_next:
        out_shape.append(jax.ShapeDtypeStruct((mp + ms, d), jnp.bfloat16))
        args.append(g_next.reshape(1, d))
    out = pl.pallas_call(
        functools.partial(_resid_kernel, nbp=nbp, with_next=with_next),
        grid=((mp + ms) // tr,),
        in_specs=[row, prompt, sample, vec] + ([vec] if with_next else []),
        out_specs=out_specs, out_shape=out_shape,
        compiler_params=_cparams(("arbitrary",)),
        name="resid_norm" if with_next else "resid_last",
    )(*args)
    return (out[0], out[1], out[2]) if with_next else (out[0], out[1], None)


def _mm_kernel(*refs, n_a, k_bounds, nk, relu2):
    a_refs, b_ref, o_ref, scratch = refs[:n_a], refs[n_a], refs[n_a + 1], refs[n_a + 2:]
    acc_ref = scratch[0] if scratch else o_ref
    k = pl.program_id(2)

    for t, a_ref in enumerate(a_refs):
        lo, hi = k_bounds[t], k_bounds[t + 1]

        def part(a_ref=a_ref):
            return jnp.dot(a_ref[...], b_ref[...].astype(jnp.bfloat16), preferred_element_type=jnp.float32)

        def assign(part=part):
            acc_ref[...] = part()

        def accumulate(part=part):
            acc_ref[...] += part()

        if lo == 0:
            pl.when(k == 0)(assign)
            if hi > 1:
                pl.when((k > 0) & (k < hi))(accumulate)
        else:
            pl.when((k >= lo) & (k < hi))(accumulate)

    if scratch or relu2:
        @pl.when(k == nk - 1)
        def _():
            r = acc_ref[...]
            if relu2:
                r = jnp.square(jnp.maximum(r, 0.0))
            o_ref[...] = r.astype(o_ref.dtype)


def _matmul(a_list, b, layer, *, n_cols, tn, out_dtype, relu2=False, name):
    m = a_list[0].shape[0]
    tm = _largest_tile(m, 2080, 16)
    tk = _largest_tile(math.gcd(*[a.shape[1] for a in a_list]), 1024, LANES)
    k_bounds = [0]
    for a in a_list:
        k_bounds.append(k_bounds[-1] + a.shape[1] // tk)
    nk = k_bounds[-1]
    assert nk * tk == b.shape[1] and n_cols % tn == 0

    def a_spec(lo, hi):
        return pl.BlockSpec((tm, tk), lambda i, j, k: (i, jnp.clip(k - lo, 0, hi - lo - 1)))

    scratch = [] if out_dtype == jnp.float32 else [pltpu.VMEM((tm, tn), jnp.float32)]
    return pl.pallas_call(
        functools.partial(_mm_kernel, n_a=len(a_list), k_bounds=tuple(k_bounds), nk=nk, relu2=relu2),
        grid=(m // tm, n_cols // tn, nk),
        in_specs=[a_spec(k_bounds[t], k_bounds[t + 1]) for t in range(len(a_list))]
                 + [pl.BlockSpec((None, tk, tn), lambda i, j, k: (layer, k, j))],
        out_specs=pl.BlockSpec((tm, tn), lambda i, j, k: (i, j)),
        out_shape=jax.ShapeDtypeStruct((m, n_cols), out_dtype),
        scratch_shapes=scratch,
        compiler_params=_cparams(("parallel", "parallel", "arbitrary")),
        name=name,
    )(*a_list, b)


def _mm_fullk_kernel(*refs, n_a, relu2, b_transposed, seg_starts):
    a_refs, b_ref, o_refs = refs[:n_a], refs[n_a], refs[n_a + 1:]
    acc, lo = None, 0
    for a_ref in a_refs:
        width = a_ref.shape[1]
        if b_transposed:
            part = lax.dot_general(a_ref[...], b_ref[:, lo:lo + width].astype(jnp.bfloat16),
                                   (((1,), (1,)), ((), ())), preferred_element_type=jnp.float32)
        else:
            part = jnp.dot(a_ref[...], b_ref[lo:lo + width, :].astype(jnp.bfloat16),
                           preferred_element_type=jnp.float32)
        acc = part if acc is None else acc + part
        lo += width
    if relu2:
        acc = jnp.square(jnp.maximum(acc, 0.0))
    if len(o_refs) == 1:
        o_refs[0][...] = acc.astype(o_refs[0].dtype)
    else:
        j = pl.program_id(1)
        for t, o_ref in enumerate(o_refs):
            def store(o_ref=o_ref):
                o_ref[...] = acc.astype(o_ref.dtype)
            pl.when((j >= seg_starts[t]) & (j < seg_starts[t + 1]))(store)


def _matmul_fullk(a_list, b, layer, *, segments, tn, out_dtype, relu2=False, b_transposed=False, name):
    m = a_list[0].shape[0]
    kdim = sum(a.shape[1] for a in a_list)
    tm = _largest_tile(m, 2080, 16)
    seg_starts = [0]
    for col0, n_cols in segments:
        assert n_cols % tn == 0
        seg_starts.append(seg_starts[-1] + n_cols // tn)

    def first_col(j):
        col = segments[0][0] + j * tn
        for t in range(1, len(segments)):
            col = jnp.where(j >= seg_starts[t], segments[t][0] + (j - seg_starts[t]) * tn, col)
        return col

    if b_transposed:
        assert kdim == b.shape[2] and all(c % SUBLANES == 0 for c, _ in segments)
        b_spec = pl.BlockSpec((None, pl.Element(tn), pl.Element(kdim)),
                              lambda i, j: (layer, pl.multiple_of(first_col(j), SUBLANES), 0))
    else:
        assert kdim == b.shape[1] and all(c % tn == 0 for c, _ in segments)
        b_spec = pl.BlockSpec((None, kdim, tn), lambda i, j: (layer, 0, first_col(j) // tn))

    def out_spec(t):
        lo, n = seg_starts[t], seg_starts[t + 1] - seg_starts[t]
        return pl.BlockSpec((tm, tn), lambda i, j: (i, jnp.clip(j - lo, 0, n - 1)))

    outs = pl.pallas_call(
        functools.partial(_mm_fullk_kernel, n_a=len(a_list), relu2=relu2, b_transposed=b_transposed,
                          seg_starts=tuple(seg_starts)),
        grid=(m // tm, seg_starts[-1]),
        in_specs=[pl.BlockSpec((tm, a.shape[1]), lambda i, j: (i, 0), pipeline_mode=pl.Buffered(1))
                  for a in a_list] + [b_spec],
        out_specs=[out_spec(t) for t in range(len(segments))],
        out_shape=[jax.ShapeDtypeStruct((m, n_cols), out_dtype) for _, n_cols in segments],
        compiler_params=_cparams(("parallel", "arbitrary")),
        name=name,
    )(*a_list, b)
    return outs[0] if len(segments) == 1 else outs


def _rope_tables(pos0, length):
    half = HEAD_DIM // 2
    inv_freq = ROPE_THETA ** (-jnp.arange(half, dtype=jnp.float32) / half)
    pos = (pos0 + jnp.arange(length, dtype=jnp.int32)).astype(jnp.float32)
    ang = pos[:, None] * inv_freq[None, :]
    cos, sin = jnp.cos(ang), jnp.sin(ang)
    reps = LANES // HEAD_DIM
    cos_t = jnp.tile(jnp.concatenate([cos, cos], axis=-1), (1, reps))
    sin_t = jnp.tile(jnp.concatenate([-sin, sin], axis=-1), (1, reps))
    return cos_t, sin_t


def _rope(x, cos_t, sin_t):
    half = HEAD_DIM // 2
    lane = lax.broadcasted_iota(jnp.int32, x.shape, 1)
    swapped = jnp.where((lane % HEAD_DIM) < half,
                        pltpu.roll(x, LANES - half, 1), pltpu.roll(x, half, 1))
    return x * cos_t + swapped * sin_t


def _rope_wide(x, cos_t, sin_t):
    return jnp.concatenate(
        [_rope(x[:, c * LANES:(c + 1) * LANES], cos_t, sin_t) for c in range(x.shape[1] // LANES)], axis=-1)


def _attend(q_rot, k_ext, v_ext, mask_fn, sink_ref, stack):
    rq, s_len = q_rot.shape[0], k_ext.shape[0]
    kb = k_ext.astype(jnp.bfloat16)
    vb = v_ext.astype(jnp.bfloat16)
    scale = HEAD_DIM ** -0.5
    rows = lax.broadcasted_iota(jnp.int32, (stack * rq, s_len), 0)
    mask = mask_fn(rows % rq, lax.broadcasted_iota(jnp.int32, (stack * rq, s_len), 1))
    outs = []
    for h0 in range(0, N_Q_HEADS, stack):
        g = h0 // Q_PER_KV
        kg = kb[:, g * HEAD_DIM:(g + 1) * HEAD_DIM]
        vg = vb[:, g * HEAD_DIM:(g + 1) * HEAD_DIM]
        heads = range(h0, h0 + stack)
        qg = jnp.concatenate([q_rot[:, h * HEAD_DIM:(h + 1) * HEAD_DIM] for h in heads], axis=0)
        sink = jnp.concatenate([jnp.full((rq, 1), sink_ref[0, h], jnp.float32) for h in heads], axis=0)
        s = lax.dot_general((qg * scale).astype(jnp.bfloat16), kg, (((1,), (1,)), ((), ())),
                            preferred_element_type=jnp.float32)
        s = jnp.where(mask, s, -jnp.inf)
        m = jnp.maximum(jnp.max(s, axis=-1, keepdims=True), sink)
        p = jnp.exp(s - m)
        denom = jnp.sum(p, axis=-1, keepdims=True) + jnp.exp(sink - m)
        o = jnp.dot(p.astype(jnp.bfloat16), vg, preferred_element_type=jnp.float32) / denom
        outs.extend(o[hh * rq:(hh + 1) * rq] for hh in range(stack))
    return jnp.concatenate(outs, axis=-1)


def _fill_rows(sample_rows, block_rows):
    ms = sample_rows.shape[0]
    padded = -(-ms // block_rows) * block_rows
    return jnp.pad(sample_rows.astype(jnp.bfloat16), ((0, padded - ms), (0, 0)))


def _attn_prompt_kernel(sink_ref, q_ref, kc_ref, kp_ref, vc_ref, vp_ref,
                        cosc_ref, sinc_ref, cosp_ref, sinp_ref, fill_ref,
                        o_ref, kwin_ref, vwin_ref, *, nb, nbp):
    i = pl.program_id(0)
    j = i % nb

    @pl.when(i < nbp)
    def _():
        cosc, sinc = cosc_ref[...], sinc_ref[...]
        q_rot = _rope_wide(q_ref[...], cosc, sinc)
        k_cur = _rope(kc_ref[...], cosc, sinc)
        k_prev = _rope(kp_ref[...], cosp_ref[...], sinp_ref[...])
        k_ext = jnp.concatenate([k_prev, k_cur], axis=0)
        v_ext = jnp.concatenate([vp_ref[...], vc_ref[...]], axis=0)

        def mask_fn(tq, sk):
            return ((sk >= tq) & (sk < WINDOW) & (j > 0)) | ((sk >= WINDOW) & ((sk - WINDOW) <= tq))

        o_ref[...] = _attend(q_rot, k_ext, v_ext, mask_fn, sink_ref, 1).astype(o_ref.dtype)

        @pl.when(j == nb - 1)
        def _():
            kwin_ref[0] = k_cur
            vwin_ref[0] = vc_ref[...]

    @pl.when(i >= nbp)
    def _():
        o_ref[...] = fill_ref[...]


def _attn_prompt(z1, sinks, a_sample, batch, seq):
    nb = seq // WINDOW
    nbp = batch * nb
    fill = _fill_rows(a_sample, WINDOW)
    nbs = fill.shape[0] // WINDOW
    cos_t, sin_t = _rope_tables(0, seq)
    cur = lambda i: jnp.minimum(i, nbp - 1)
    prev = lambda i: cur(i) - (cur(i) % nb > 0)
    blk = lambda col: pl.BlockSpec((WINDOW, LANES), lambda i: (cur(i), col))
    blk_prev = lambda col: pl.BlockSpec((WINDOW, LANES), lambda i: (prev(i), col))
    tab = pl.BlockSpec((WINDOW, LANES), lambda i: (cur(i) % nb, 0))
    tab_prev = pl.BlockSpec((WINDOW, LANES), lambda i: (prev(i) % nb, 0))
    win = pl.BlockSpec((1, WINDOW, KV_WIDTH), lambda i: (cur(i) // nb, 0, 0))
    return pl.pallas_call(
        functools.partial(_attn_prompt_kernel, nb=nb, nbp=nbp),
        grid=(nbp + nbs,),
        in_specs=[pl.BlockSpec(memory_space=pltpu.SMEM),
                  pl.BlockSpec((WINDOW, ATT_WIDTH), lambda i: (cur(i), 0)),
                  blk(COL_KA // LANES), blk_prev(COL_KA // LANES),
                  blk(COL_VA // LANES), blk_prev(COL_VA // LANES),
                  tab, tab, tab_prev, tab_prev,
                  pl.BlockSpec((WINDOW, ATT_WIDTH), lambda i: (jnp.maximum(i - nbp, 0), 0))],
        out_specs=[pl.BlockSpec((WINDOW, ATT_WIDTH), lambda i: (i, 0)), win, win],
        out_shape=[jax.ShapeDtypeStruct((z1.shape[0], ATT_WIDTH), jnp.bfloat16),
                   jax.ShapeDtypeStruct((batch, WINDOW, KV_WIDTH), jnp.float32),
                   jax.ShapeDtypeStruct((batch, WINDOW, KV_WIDTH), jnp.float32)],
        compiler_params=_cparams(("arbitrary",)),
        name="attn_prompt",
    )(sinks.reshape(1, N_Q_HEADS), z1, z1, z1, z1, z1, cos_t, sin_t, cos_t, sin_t, fill)


def _attn_sample_kernel(sink_ref, z_ref, ck_ref, cv_ref, cos_ref, sin_ref,
                        o_ref, kout_ref, vout_ref, kext_ref, vext_ref, *, n_new, rows):
    cos_t, sin_t = cos_ref[...], sin_ref[...]
    z = z_ref[0]
    q_rot = _rope_wide(z[:, COL_QA:COL_QA + ATT_WIDTH], cos_t, sin_t)
    k_new = _rope(z[:, COL_KA:COL_KA + KV_WIDTH], cos_t, sin_t)
    v_new = z[:, COL_VA:COL_VA + KV_WIDTH]
    kext_ref[0:WINDOW, :] = ck_ref[0]
    vext_ref[0:WINDOW, :] = cv_ref[0]
    kext_ref[WINDOW:WINDOW + rows, :] = k_new
    vext_ref[WINDOW:WINDOW + rows, :] = v_new
    fill = jnp.zeros((WINDOW - rows, KV_WIDTH), jnp.float32)
    kext_ref[WINDOW + rows:, :] = fill
    vext_ref[WINDOW + rows:, :] = fill

    def mask_fn(tq, sk):
        return ((sk >= tq) & (sk < WINDOW)) | ((sk >= WINDOW) & ((sk - WINDOW) <= tq) & (sk < WINDOW + n_new))

    o_ref[0] = _attend(q_rot, kext_ref[...], vext_ref[...], mask_fn, sink_ref, Q_PER_KV)
    kout_ref[0] = kext_ref[pl.ds(n_new, WINDOW), :]
    vout_ref[0] = vext_ref[pl.ds(n_new, WINDOW), :]


def _attn_sample(zs, cache_k, cache_v, sinks, n_new):
    s, rows, _ = zs.shape
    cos_t, sin_t = _rope_tables(PAST_LEN, rows)
    seq_blk = lambda r, c: pl.BlockSpec((1, r, c), lambda n: (n, 0, 0))
    tab = pl.BlockSpec((rows, LANES), lambda n: (0, 0))
    return pl.pallas_call(
        functools.partial(_attn_sample_kernel, n_new=n_new, rows=rows),
        grid=(s,),
        in_specs=[pl.BlockSpec(memory_space=pltpu.SMEM),
                  seq_blk(rows, Z1_COLS), seq_blk(WINDOW, KV_WIDTH), seq_blk(WINDOW, KV_WIDTH), tab, tab],
        out_specs=[seq_blk(rows, ATT_WIDTH), seq_blk(WINDOW, KV_WIDTH), seq_blk(WINDOW, KV_WIDTH)],
        out_shape=[jax.ShapeDtypeStruct((s, rows, ATT_WIDTH), jnp.float32),
                   jax.ShapeDtypeStruct((s, WINDOW, KV_WIDTH), jnp.float32),
                   jax.ShapeDtypeStruct((s, WINDOW, KV_WIDTH), jnp.float32)],
        scratch_shapes=[pltpu.VMEM((2 * WINDOW, KV_WIDTH), jnp.float32),
                        pltpu.VMEM((2 * WINDOW, KV_WIDTH), jnp.float32)],
        compiler_params=_cparams(("parallel",)),
        name="attn_sample",
    )(sinks.reshape(1, N_Q_HEADS), zs, cache_k, cache_v, cos_t, sin_t)


def _log_sigmoid(x):
    return jnp.minimum(x, 0.0) - jnp.log(1.0 + jnp.exp(-jnp.abs(x)))


def _mlstm_chunk(q, k, v, li_col, lf_col, li_row, lf_row, c_state, n_state, m_state):
    t = q.shape[0]
    r = lax.broadcasted_iota(jnp.int32, (t, t), 0)
    c = lax.broadcasted_iota(jnp.int32, (t, t), 1)
    tril = c <= r
    b_col = jnp.sum(jnp.where(tril, lf_row, 0.0), axis=1, keepdims=True)
    b_row = jnp.sum(jnp.where(r <= c, lf_col, 0.0), axis=0, keepdims=True)
    b_last = jnp.sum(lf_row, axis=1, keepdims=True)
    log_d = jnp.where(tril, b_col - b_row + li_row, -jnp.inf)
    inter = b_col + m_state
    m_row = jnp.maximum(inter, jnp.max(log_d, axis=1, keepdims=True))
    qb = q.astype(jnp.bfloat16)
    kb = k.astype(jnp.bfloat16)
    vb = v.astype(jnp.bfloat16)
    qk = lax.dot_general(qb, kb, (((1,), (1,)), ((), ())), preferred_element_type=jnp.float32)
    s = qk * jnp.exp(log_d - m_row)
    w_inter = jnp.exp(inter - m_row)
    cq = lax.dot_general(qb, c_state.astype(jnp.bfloat16), (((1,), (1,)), ((), ())),
                         preferred_element_type=jnp.float32)
    num = jnp.dot(s.astype(jnp.bfloat16), vb, preferred_element_type=jnp.float32) + w_inter * cq
    den = jnp.sum(s, axis=1, keepdims=True) + w_inter * jnp.sum(q * n_state, axis=1, keepdims=True)
    h = num / jnp.maximum(jnp.abs(den), jnp.exp(-m_row))
    g_row = b_last - b_row + li_row
    g_col = b_last - b_col + li_col
    m_new = jnp.maximum(b_last + m_state, jnp.max(g_row, axis=1, keepdims=True))
    decay = jnp.exp(b_last + m_state - m_new)
    wk_col = jnp.exp(g_col - m_new)
    wv_t = (wk_col * v).T.astype(jnp.bfloat16)
    c_new = decay * c_state + jnp.dot(wv_t, kb, preferred_element_type=jnp.float32)
    n_new = decay * n_state + jnp.sum(wk_col * k, axis=0, keepdims=True)
    return h, c_new, n_new, m_new


def _mlstm_heads(bias_ref, z, og, zgt, gm_ref, c_s, n_s, m_s, write_out, *, t, n_valid):
    rows = z.shape[0]
    zg = z[:, COL_GATE:COL_GATE + LANES]
    if rows < t:
        pad = lambda a: jnp.concatenate([a, jnp.zeros((t - rows, a.shape[1]), a.dtype)], axis=0)
        z, og, zg = pad(z), pad(og), pad(zg)
    valid_col = lax.broadcasted_iota(jnp.int32, (t, 1), 0) < n_valid
    valid_row = lax.broadcasted_iota(jnp.int32, (1, t), 1) < n_valid
    for h in range(N_MLSTM_HEADS):
        q = z[:, COL_QM + h * MLSTM_DK:COL_QM + (h + 1) * MLSTM_DK]
        k = z[:, COL_KM + h * MLSTM_DK:COL_KM + (h + 1) * MLSTM_DK] * (MLSTM_DK ** -0.5)
        v = z[:, COL_VM + h * MLSTM_DV:COL_VM + (h + 1) * MLSTM_DV]
        b_i = bias_ref[0, h]
        b_f = bias_ref[1, h]
        li_col = zg[:, h:h + 1] + b_i
        lf_col = _log_sigmoid(zg[:, N_MLSTM_HEADS + h:N_MLSTM_HEADS + h + 1] + b_f)
        li_row = zgt[h:h + 1, :] + b_i
        lf_row = _log_sigmoid(zgt[N_MLSTM_HEADS + h:N_MLSTM_HEADS + h + 1, :] + b_f)
        if n_valid < t:
            li_col = jnp.where(valid_col, li_col, -jnp.inf)
            lf_col = jnp.where(valid_col, lf_col, 0.0)
            li_row = jnp.where(valid_row, li_row, -jnp.inf)
            lf_row = jnp.where(valid_row, lf_row, 0.0)
        hm, c_new, n_new, m_new = _mlstm_chunk(
            q, k, v, li_col, lf_col, li_row, lf_row, c_s[h], n_s[h:h + 1, :], m_s[h:h + 1, 0:1])
        c_s[h] = c_new
        n_s[h:h + 1, :] = n_new
        m_s[h:h + 1, :] = jnp.broadcast_to(m_new, (1, LANES))
        gate = jax.nn.sigmoid(og[:, h * MLSTM_DV:(h + 1) * MLSTM_DV])
        write_out(h, (_rms(hm, gm_ref[h:h + 1, :]) * gate)[:rows])


def _mlstm_prompt_kernel(bias_ref, z_ref, og_ref, zgt_ref, gm_ref, fill_ref,
                         o_ref, co_ref, no_ref, mo_ref, c_s, n_s, m_s, *, t, nc, nbp):
    i = pl.program_id(0)
    step = i % nc

    @pl.when(i < nbp)
    def _():
        @pl.when(step == 0)
        def _():
            c_s[...] = jnp.zeros_like(c_s)
            n_s[...] = jnp.zeros_like(n_s)
            m_s[...] = jnp.zeros_like(m_s)

        def write_out(h, out):
            o_ref[:, h * MLSTM_DV:(h + 1) * MLSTM_DV] = out.astype(o_ref.dtype)

        _mlstm_heads(bias_ref, z_ref[...], og_ref[...], zgt_ref[...], gm_ref, c_s, n_s, m_s, write_out,
                     t=t, n_valid=t)

        @pl.when(step == nc - 1)
        def _():
            co_ref[0] = c_s[...]
            no_ref[0] = n_s[...]
            mo_ref[0] = m_s[...]

    @pl.when(i >= nbp)
    def _():
        o_ref[...] = fill_ref[...]


def _mlstm_sample_kernel(bias_ref, z_ref, og_ref, zgt_ref, gm_ref, c0_ref, n0_ref, m0_ref, *rest,
                         t, n_valid, layer, first_call):
    o_ref, co_ref, no_ref, mo_ref, c_s, n_s, m_s = rest if first_call else rest[1:]
    c_s[...] = c0_ref[0]
    n_s[...] = n0_ref[0]
    m_s[...] = m0_ref[0]

    def write_out(h, out):
        o_ref[0, :, h * MLSTM_DV:(h + 1) * MLSTM_DV] = out

    _mlstm_heads(bias_ref, z_ref[0], og_ref[0], zgt_ref[0], gm_ref, c_s, n_s, m_s, write_out,
                 t=t, n_valid=n_valid)
    if first_call:
        for other in range(co_ref.shape[0]):
            co_ref[other, 0] = c_s[...] if other == layer else jnp.zeros_like(c_s)
    else:
        co_ref[0] = c_s[...]
    no_ref[0] = n_s[...]
    mo_ref[0] = m_s[...]


_MLSTM_SCRATCH = [pltpu.VMEM((N_MLSTM_HEADS, MLSTM_DV, MLSTM_DK), jnp.float32),
                  pltpu.VMEM((N_MLSTM_HEADS, MLSTM_DK), jnp.float32),
                  pltpu.VMEM((N_MLSTM_HEADS, LANES), jnp.float32)]


def _mlstm_state_shapes(n):
    return [jax.ShapeDtypeStruct((n, N_MLSTM_HEADS, MLSTM_DV, MLSTM_DK), jnp.float32),
            jax.ShapeDtypeStruct((n, N_MLSTM_HEADS, MLSTM_DK), jnp.float32),
            jax.ShapeDtypeStruct((n, N_MLSTM_HEADS, LANES), jnp.float32)]


def _mlstm_state_specs(index):
    return [pl.BlockSpec((1, N_MLSTM_HEADS, MLSTM_DV, MLSTM_DK), lambda i: (index(i), 0, 0, 0)),
            pl.BlockSpec((1, N_MLSTM_HEADS, MLSTM_DK), lambda i: (index(i), 0, 0)),
            pl.BlockSpec((1, N_MLSTM_HEADS, LANES), lambda i: (index(i), 0, 0))]


def _mlstm_prompt(z1, ztail, zgt, bias, g_mlstm, b_sample, batch, seq):
    t = MLSTM_T_PROMPT
    nc = seq // t
    nbp = batch * nc
    fill = _fill_rows(b_sample, t)
    nbs = fill.shape[0] // t
    cur = lambda i: jnp.minimum(i, nbp - 1)
    row = lambda w: pl.BlockSpec((t, w), lambda i: (cur(i), 0))
    return pl.pallas_call(
        functools.partial(_mlstm_prompt_kernel, t=t, nc=nc, nbp=nbp),
        grid=(nbp + nbs,),
        in_specs=[pl.BlockSpec(memory_space=pltpu.SMEM),
                  row(Z1_COLS), row(MLSTM_WIDTH),
                  pl.BlockSpec((SUBLANES, t), lambda i: (0, cur(i))),
                  pl.BlockSpec((N_MLSTM_HEADS, MLSTM_DV), lambda i: (0, 0)),
                  pl.BlockSpec((t, MLSTM_WIDTH), lambda i: (jnp.maximum(i - nbp, 0), 0))],
        out_specs=[pl.BlockSpec((t, MLSTM_WIDTH), lambda i: (i, 0))] + _mlstm_state_specs(lambda i: cur(i) // nc),
        out_shape=[jax.ShapeDtypeStruct((z1.shape[0], MLSTM_WIDTH), jnp.bfloat16)] + _mlstm_state_shapes(batch),
        scratch_shapes=_MLSTM_SCRATCH,
        compiler_params=_cparams(("arbitrary",)),
        name="mlstm_prompt",
    )(bias, z1, ztail, zgt, g_mlstm, fill)


def _mlstm_sample(zs, zts, zgts, bias, g_mlstm, c0_all, n0, m0, n_valid, layer, c_out_prev):
    s, rows, _ = zs.shape
    t = MLSTM_T_SAMPLE
    depth = c0_all.shape[0]
    first_call = c_out_prev is None
    seq_blk = lambda r, c: pl.BlockSpec((1, r, c), lambda n: (n, 0, 0))
    c_in = pl.BlockSpec((None, 1, N_MLSTM_HEADS, MLSTM_DV, MLSTM_DK), lambda n: (layer, n, 0, 0, 0))
    c_out = (pl.BlockSpec((depth, 1, N_MLSTM_HEADS, MLSTM_DV, MLSTM_DK), lambda n: (0, n, 0, 0, 0))
             if first_call else c_in)
    small = _mlstm_state_specs(lambda n: n)[1:]
    in_specs = [pl.BlockSpec(memory_space=pltpu.SMEM),
                seq_blk(rows, Z1_COLS), seq_blk(rows, MLSTM_WIDTH), seq_blk(SUBLANES, t),
                pl.BlockSpec((N_MLSTM_HEADS, MLSTM_DV), lambda n: (0, 0)), c_in] + small
    args = [bias, zs, zts, zgts, g_mlstm, c0_all, n0, m0]
    aliases = {}
    if not first_call:
        in_specs.append(pl.BlockSpec(memory_space=pl.ANY))
        args.append(c_out_prev)
        aliases = {len(args) - 1: 1}
    return pl.pallas_call(
        functools.partial(_mlstm_sample_kernel, t=t, n_valid=n_valid, layer=layer, first_call=first_call),
        grid=(s,),
        in_specs=in_specs,
        out_specs=[seq_blk(rows, MLSTM_WIDTH), c_out] + small,
        out_shape=[jax.ShapeDtypeStruct((s, rows, MLSTM_WIDTH), jnp.float32),
                   jax.ShapeDtypeStruct((depth,) + c0_all.shape[1:], jnp.float32)] + _mlstm_state_shapes(s)[1:],
        input_output_aliases=aliases,
        scratch_shapes=_MLSTM_SCRATCH,
        compiler_params=_cparams(("arbitrary",)),
        name="mlstm_sample",
    )(*args)


def _conv_norm_act(cv, cb, g, b):
    cv = cv + cb
    mu = jnp.mean(cv, axis=-1, keepdims=True)
    d = cv - mu
    var = jnp.mean(d * d, axis=-1, keepdims=True)
    y = d * lax.rsqrt(var + NORM_EPS) * g + b
    return y * jax.nn.sigmoid(y)


def _conv_prompt_kernel(ga_ref, gb_ref, w_ref, cb_ref, g_ref, b_ref, fill_ref, o_ref, st_ref, ext_ref, cv_ref,
                        *, nt, nbp):
    tb, rsub = CONV_TB, CONV_RSUB
    i = pl.program_id(0)
    step = i % nt

    @pl.when(i < nbp)
    def _():
        @pl.when(step == 0)
        def _():
            ext_ref[0:CONV_HIST, :] = jnp.zeros((CONV_HIST, CONV_CH), jnp.float32)

        ext_ref[CONV_HIST:, :] = ga_ref[...] * jax.nn.sigmoid(gb_ref[...])
        off = CONV_HIST - (CONV_WIDTH - 1)
        span = rsub + CONV_HIST
        for lc in range(CONV_CH // LANES):
            cols = slice(lc * LANES, (lc + 1) * LANES)
            for r0 in range(0, tb, rsub):
                e = ext_ref[r0:r0 + span, cols]
                acc = None
                for s in range(SUBLANES):
                    zs = e if s == 0 else pltpu.roll(e, span - s, 0)
                    for a in range(CONV_HIST // SUBLANES + 1):
                        j = SUBLANES * a + s - off
                        if 0 <= j < CONV_WIDTH:
                            term = w_ref[j:j + 1, cols] * zs[SUBLANES * a:SUBLANES * a + rsub]
                            acc = term if acc is None else acc + term
                cv_ref[r0:r0 + rsub, cols] = acc
        o_ref[...] = _conv_norm_act(cv_ref[...], cb_ref[...], g_ref[...], b_ref[...]).astype(o_ref.dtype)

        @pl.when(step == nt - 1)
        def _():
            st_ref[0] = ext_ref[tb + off:tb + CONV_HIST, :]

        ext_ref[0:CONV_HIST, :] = ext_ref[tb:tb + CONV_HIST, :]

    @pl.when(i >= nbp)
    def _():
        o_ref[...] = fill_ref[...]


def _conv_prompt(ztail, conv_w, conv_b, g_conv, b_conv, c_sample, batch, seq):
    tb = CONV_TB
    nt = seq // tb
    nbp = batch * nt
    fill = _fill_rows(c_sample, tb)
    nbs = fill.shape[0] // tb
    cur = lambda i: jnp.minimum(i, nbp - 1)
    vec = pl.BlockSpec((1, CONV_CH), lambda i: (0, 0))
    row = lambda col: pl.BlockSpec((tb, CONV_CH), lambda i: (cur(i), col))
    return pl.pallas_call(
        functools.partial(_conv_prompt_kernel, nt=nt, nbp=nbp),
        grid=(nbp + nbs,),
        in_specs=[row(MLSTM_WIDTH // CONV_CH), row(MLSTM_WIDTH // CONV_CH + 1),
                  pl.BlockSpec((CONV_WIDTH, CONV_CH), lambda i: (0, 0)), vec, vec, vec,
                  pl.BlockSpec((tb, CONV_CH), lambda i: (jnp.maximum(i - nbp, 0), 0))],
        out_specs=[pl.BlockSpec((tb, CONV_CH), lambda i: (i, 0)),
                   pl.BlockSpec((1, CONV_WIDTH - 1, CONV_CH), lambda i: (cur(i) // nt, 0, 0))],
        out_shape=[jax.ShapeDtypeStruct((ztail.shape[0], CONV_CH), jnp.bfloat16),
                   jax.ShapeDtypeStruct((batch, CONV_WIDTH - 1, CONV_CH), jnp.float32)],
        scratch_shapes=[pltpu.VMEM((tb + CONV_HIST, CONV_CH), jnp.float32),
                        pltpu.VMEM((tb, CONV_CH), jnp.float32)],
        compiler_params=_cparams(("arbitrary",)),
        name="conv_prompt",
    )(ztail, ztail, conv_w, conv_b.reshape(1, -1), g_conv.reshape(1, -1), b_conv.reshape(1, -1), fill)


def _conv_sample_kernel(ga_ref, gb_ref, st_ref, w_ref, cb_ref, g_ref, b_ref, o_ref, sto_ref, ext_ref, *, n_new):
    hist = CONV_WIDTH - 1
    ext_ref[0:hist] = st_ref[...]
    ext_ref[hist:] = ga_ref[...] * jax.nn.sigmoid(gb_ref[...])
    for t in range(n_new):
        acc = w_ref[0:1, :] * ext_ref[t]
        for j in range(1, CONV_WIDTH):
            acc = acc + w_ref[j:j + 1, :] * ext_ref[t + j]
        o_ref[t] = _conv_norm_act(acc, cb_ref[...], g_ref[...], b_ref[...])
    sto_ref[...] = ext_ref[n_new:]


def _conv_sample(ga, gb, state, conv_w, conv_b, g_conv, b_conv):
    n_new, s, ch = ga.shape
    hist = CONV_WIDTH - 1
    full = lambda shape: pl.BlockSpec(shape, lambda i: (0,) * len(shape))
    return pl.pallas_call(
        functools.partial(_conv_sample_kernel, n_new=n_new),
        grid=(1,),
        in_specs=[full((n_new, s, ch)), full((n_new, s, ch)), full((hist, s, ch)),
                  full((CONV_WIDTH, ch)), full((1, ch)), full((1, ch)), full((1, ch))],
        out_specs=[full((n_new, s, ch)), full((hist, s, ch))],
        out_shape=[jax.ShapeDtypeStruct((n_new, s, ch), jnp.float32),
                   jax.ShapeDtypeStruct((hist, s, ch), jnp.float32)],
        scratch_shapes=[pltpu.VMEM((hist + n_new, s, ch), jnp.float32)],
        compiler_params=_cparams(("arbitrary",)),
        name="conv_sample",
    )(ga, gb, state, conv_w, conv_b.reshape(1, -1), g_conv.reshape(1, -1), b_conv.reshape(1, -1))


def _layer(l, xp, xs, h, batch, seq, s_batch, s_len, ck, cv, c0_all, c_out_prev, n0, m0, conv_state,
           weights, vectors, g_next):
    w_in_t, w_out, w_up, w_down = weights
    (sinks, b_ig, b_fg, g_mlstm, conv_w, conv_b, g_conv, b_conv, g_post_mix, g_pre_mlp, g_post_mlp) = vectors
    mp = batch * seq
    rows = SUBLANES
    assert s_len <= rows
    f32, bf16 = jnp.float32, jnp.bfloat16

    z1, zt = _matmul_fullk([h], w_in_t, l, segments=[(0, Z1_COLS), (COL_OG, TAIL_COLS)], tn=2 * MXU_COLS,
                           out_dtype=f32, b_transposed=True, name="proj_in")
    gates = z1[:, COL_GATE:COL_OG]
    zgt = gates.T
    bias = jnp.stack([b_ig, b_fg])

    def pad_rows(a):
        a = a.reshape(s_batch, s_len, a.shape[-1])
        return jnp.pad(a, ((0, 0), (0, rows - s_len), (0, 0)))

    zs, zts = pad_rows(z1[mp:]), pad_rows(zt[mp:])
    zgts = jnp.pad(gates[mp:].reshape(s_batch, s_len, 2 * N_MLSTM_HEADS).transpose(0, 2, 1),
                   ((0, 0), (0, 0), (0, MLSTM_T_SAMPLE - s_len)))
    unpad = lambda a: a[:, :s_len].reshape(s_batch * s_len, a.shape[-1])

    a_s, kwin_s, vwin_s = _attn_sample(zs, ck.reshape(s_batch, WINDOW, KV_WIDTH),
                                       cv.reshape(s_batch, WINDOW, KV_WIDTH), sinks, s_len)
    a_mix, kwin_p, vwin_p = _attn_prompt(z1, sinks, unpad(a_s), batch, seq)
    m0b = jnp.broadcast_to(m0[:, :, None], m0.shape + (LANES,))
    b_s, c_s_all, n_s, m_s = _mlstm_sample(zs, zts, zgts, bias, g_mlstm, c0_all, n0, m0b, s_len, l, c_out_prev)
    b_mix, c_p, n_p, m_p = _mlstm_prompt(z1, zt, zgt, bias, g_mlstm, unpad(b_s), batch, seq)
    tm = lambda a: a.reshape(s_batch, s_len, CONV_CH).transpose(1, 0, 2)
    ga_s = tm(zt[mp:, MLSTM_WIDTH:MLSTM_WIDTH + CONV_CH])
    gb_s = tm(zt[mp:, MLSTM_WIDTH + CONV_CH:])
    c_sm, st_s = _conv_sample(ga_s, gb_s, conv_state.transpose(1, 0, 2), conv_w, conv_b, g_conv, b_conv)
    c_mix, st_p = _conv_prompt(zt, conv_w, conv_b, g_conv, b_conv,
                               c_sm.transpose(1, 0, 2).reshape(s_batch * s_len, CONV_CH), batch, seq)

    y = _matmul_fullk([a_mix, b_mix, c_mix], w_out, l, segments=[(0, D_MODEL)], tn=2 * MXU_COLS, out_dtype=f32,
                      name="proj_out")
    xp, xs, h2 = _resid(y, xp, xs, g_post_mix, g_pre_mlp)
    ff = _matmul_fullk([h2], w_up, l, segments=[(0, w_up.shape[2])], tn=2 * MXU_COLS, out_dtype=bf16,
                       relu2=True, name="mlp_up")
    y2 = _matmul([ff], w_down, l, n_cols=D_MODEL, tn=4 * MXU_COLS, out_dtype=f32, name="mlp_down")
    xp, xs, h_next = _resid(y2, xp, xs, g_post_mlp, g_next)

    kv = lambda a, n: a.reshape(n, WINDOW, N_KV_HEADS, HEAD_DIM)
    prompt_state = (kv(kwin_p, batch), kv(vwin_p, batch), c_p, n_p, m_p[:, :, 0], st_p)
    sample_state = (kv(kwin_s, s_batch), kv(vwin_s, s_batch), n_s, m_s[:, :, 0], st_s.transpose(1, 0, 2))
    return xp, xs, h_next, prompt_state, sample_state, c_s_all


def kernel(x_prompt, x_sample, cache_win_k, cache_win_v, state_mlstm_C, state_mlstm_n, state_mlstm_m, state_conv,
           g_pre_mix, w_in, attn_sinks, b_igate, b_fgate, g_mlstm, conv_w, conv_b, g_conv, b_conv,
           w_out, g_post_mix, g_pre_mlp, w_up, w_down, g_post_mlp):
    batch, seq, d = x_prompt.shape
    s_batch, s_len, _ = x_sample.shape
    depth = w_in.shape[0]
    xp = x_prompt.reshape(batch * seq, d)
    xs = x_sample.reshape(s_batch * s_len, d)
    h = _norm(xp, xs, g_pre_mix[0])
    w_in_t = jnp.swapaxes(w_in, 1, 2)
    p_states, s_states, s_c = [], [], None
    for l in range(depth):
        vectors = (attn_sinks[l], b_igate[l], b_fgate[l], g_mlstm[l], conv_w[l], conv_b[l],
                   g_conv[l], b_conv[l], g_post_mix[l], g_pre_mlp[l], g_post_mlp[l])
        g_next = g_pre_mix[l + 1] if l + 1 < depth else None
        xp, xs, h, ps, ss, s_c = _layer(l, xp, xs, h, batch, seq, s_batch, s_len, cache_win_k[l], cache_win_v[l],
                                        state_mlstm_C, s_c, state_mlstm_n[l], state_mlstm_m[l], state_conv[l],
                                        (w_in_t, w_out, w_up, w_down), vectors, g_next)
        p_states.append(ps)
        s_states.append(ss)
    stack = lambda states, i: jnp.stack([st[i] for st in states])
    s_k, s_v, s_n, s_m, s_cv = (stack(s_states, i) for i in range(5))
    return ((xp.reshape(batch, seq, d), xs.reshape(s_batch, s_len, d))
            + tuple(stack(p_states, i) for i in range(6))
            + (s_k, s_v, s_c, s_n, s_m, s_cv))
```
